```python
import jax, jax.numpy as jnp
from jax import lax
import numpy as np

D_MODEL = 1024
BATCH = 4
SEQ = 8192
DEPTH = 1

MIX_WIDTH = D_MODEL
POOL_WIDTH = D_MODEL // 4
POOL_WINDOWS = (2, 4, 8, 16)
POOL_GROUP = POOL_WIDTH // len(POOL_WINDOWS)
HEAD_DIM = 64
ATTN_WIDTH = MIX_WIDTH - POOL_WIDTH
N_HEADS = ATTN_WIDTH // HEAD_DIM
DILATED_CONFIGS = ((128, 1), (512, 4), (2048, 16))
BLOCK = 128
ROPE_THETA = 10000.0
IN_WIDTH = POOL_WIDTH + 3 * ATTN_WIDTH
_FF_RAW = -(-8 * D_MODEL // 3)
D_FF = ((_FF_RAW + 255) // 256) * 256
EPS = 1e-6

kernel_name = "hybrid_pool_dilated_attn_block"


def rms_norm(x, g):
    xf = x.astype(jnp.float32)
    y = xf * lax.rsqrt(jnp.mean(xf * xf, axis=-1, keepdims=True) + EPS)
    return (y * g.astype(jnp.float32)).astype(x.dtype)


def rope(x, pos):
    half = x.shape[-1] // 2
    freqs = ROPE_THETA ** (-jnp.arange(half, dtype=jnp.float32) * (2.0 / x.shape[-1]))
    ang = pos.astype(jnp.float32)[:, None] * freqs[None, :]
    cos = jnp.cos(ang)[None, :, None, :]
    sin = jnp.sin(ang)[None, :, None, :]
    xf = x.astype(jnp.float32)
    x1, x2 = xf[..., :half], xf[..., half:]
    out = jnp.concatenate([x1 * cos - x2 * sin, x2 * cos + x1 * sin], axis=-1)
    return out.astype(x.dtype)


def multi_scale_pool(u, w_pool, pool_scale):
    B, S, _ = u.shape
    ug = u.astype(jnp.float32).reshape(B, S, len(POOL_WINDOWS), POOL_GROUP)
    csum = lax.cumsum(ug, axis=1)
    t = jnp.arange(S)
    outs = []
    for gi, win in enumerate(POOL_WINDOWS):
        cg = csum[:, :, gi]
        shifted = jnp.pad(cg, ((0, 0), (win, 0), (0, 0)))[:, :S]
        cnt = jnp.minimum(t + 1, win).astype(jnp.float32)[None, :, None]
        outs.append((cg - shifted) / cnt - ug[:, :, gi])
    d = jnp.stack(outs, axis=2)
    y = jnp.einsum('bsgc,gcd->bsgd', d, w_pool.astype(jnp.float32))
    y = y.reshape(B, S, POOL_WIDTH) * pool_scale.astype(jnp.float32)
    return y.astype(u.dtype)


def dilated_branch(q, k, v, window, dilation):
    B, S, H, Dh = q.shape
    L = S // dilation
    nb = -(-L // BLOCK)
    Lp = nb * BLOCK
    w_sub = window // dilation

    def to_sub(a):
        a = a.reshape(B, L, dilation, H, Dh).transpose(0, 2, 1, 3, 4)
        return jnp.pad(a, ((0, 0), (0, 0), (0, Lp - L), (0, 0), (0, 0)))

    qs = to_sub(q).reshape(B, dilation, nb, BLOCK, H, Dh)
    kp = jnp.pad(to_sub(k), ((0, 0), (0, 0), (BLOCK, 0), (0, 0), (0, 0)))
    vp = jnp.pad(to_sub(v), ((0, 0), (0, 0), (BLOCK, 0), (0, 0), (0, 0)))

    def band(a):
        prev = a[:, :, :Lp].reshape(B, dilation, nb, BLOCK, H, Dh)
        cur = a[:, :, BLOCK:].reshape(B, dilation, nb, BLOCK, H, Dh)
        return jnp.concatenate([prev, cur], axis=3)

    kb, vb = band(kp), band(vp)
    scale = 1.0 / np.sqrt(Dh).astype(np.float32)
    s = jnp.einsum('brnqhd,brnkhd->brnhqk', qs.astype(jnp.float32), kb.astype(jnp.float32)) * scale

    qi = jnp.arange(BLOCK)[:, None]
    kj = jnp.arange(2 * BLOCK)[None, :]
    dist = qi + BLOCK - kj
    blk = jnp.arange(nb)[:, None, None]
    valid = (dist >= 0) & (dist <= w_sub) & (blk * BLOCK + kj - BLOCK >= 0)
    s = jnp.where(valid[None, None, :, None], s, -jnp.inf)

    m = jnp.max(s, axis=-1, keepdims=True)
    e = jnp.exp(s - m)
    den = jnp.sum(e, axis=-1, keepdims=True)
    lse = (m + jnp.log(den))[..., 0]
    o = jnp.einsum('brnhqk,brnkhd->brnqhd', e / den, vb.astype(jnp.float32))

    o = o.reshape(B, dilation, Lp, H, Dh)[:, :, :L].transpose(0, 2, 1, 3, 4).reshape(B, S, H, Dh)
    lse = lse.transpose(0, 1, 2, 4, 3).reshape(B, dilation, Lp, H)[:, :, :L]
    lse = lse.transpose(0, 2, 1, 3).reshape(B, S, H)
    return o, lse


def dilated_attention(q, k, v):
    outs, lses = [], []
    for window, dilation in DILATED_CONFIGS:
        o, lse = dilated_branch(q, k, v, window, dilation)
        outs.append(o)
        lses.append(lse)
    w = jax.nn.softmax(jnp.stack(lses, axis=0), axis=0)
    o = jnp.sum(w[..., None] * jnp.stack(outs, axis=0), axis=0)
    return o.astype(q.dtype)


def setup_inputs(seed: int = 0) -> dict:
    key = jax.random.key(seed)
    ks = jax.random.split(key, 13)
    f32 = jnp.float32
    nrm = lambda k, shape, s: jax.random.normal(k, shape, f32) * s
    return {
        "x": jax.random.normal(ks[0], (BATCH, SEQ, D_MODEL), f32),
        "ln_pre_mix": 1.0 + nrm(ks[1], (DEPTH, D_MODEL), 0.05),
        "w_in": nrm(ks[2], (DEPTH, D_MODEL, IN_WIDTH), D_MODEL ** -0.5),
        "w_pool": nrm(ks[3], (DEPTH, len(POOL_WINDOWS), POOL_GROUP, POOL_GROUP), POOL_GROUP ** -0.5),
        "pool_scale": 1.0 + nrm(ks[4], (DEPTH, POOL_WIDTH), 0.1),
        "w_out": nrm(ks[5], (DEPTH, MIX_WIDTH, D_MODEL), MIX_WIDTH ** -0.5),
        "ln_post_mix": 1.0 + nrm(ks[6], (DEPTH, D_MODEL), 0.05),
        "ln_pre_ffn": 1.0 + nrm(ks[7], (DEPTH, D_MODEL), 0.05),
        "w_gate": nrm(ks[8], (DEPTH, D_MODEL, D_FF), D_MODEL ** -0.5),
        "w_up": nrm(ks[9], (DEPTH, D_MODEL, D_FF), D_MODEL ** -0.5),
        "w_down": nrm(ks[10], (DEPTH, D_FF, D_MODEL), D_FF ** -0.5),
        "ln_post_ffn": 1.0 + nrm(ks[11], (DEPTH, D_MODEL), 0.05),
    }


def reference(x, ln_pre_mix, w_in, w_pool, pool_scale, w_out, ln_post_mix,
              ln_pre_ffn, w_gate, w_up, w_down, ln_post_ffn):
    B, S, _ = x.shape
    pos = jnp.arange(S)
    for l in range(DEPTH):
        h = rms_norm(x, ln_pre_mix[l])
        proj = h @ w_in[l]
        u_pool = proj[..., :POOL_WIDTH]
        q = proj[..., POOL_WIDTH:POOL_WIDTH + ATTN_WIDTH].reshape(B, S, N_HEADS, HEAD_DIM)
        k = proj[..., POOL_WIDTH + ATTN_WIDTH:POOL_WIDTH + 2 * ATTN_WIDTH].reshape(B, S, N_HEADS, HEAD_DIM)
        v = proj[..., POOL_WIDTH + 2 * ATTN_WIDTH:].reshape(B, S, N_HEADS, HEAD_DIM)
        q, k = rope(q, pos), rope(k, pos)
        pool_out = multi_scale_pool(u_pool, w_pool[l], pool_scale[l])
        attn_out = dilated_attention(q, k, v).reshape(B, S, ATTN_WIDTH)
        mix = jnp.concatenate([pool_out, attn_out], axis=-1) @ w_out[l]
        x = x + rms_norm(mix, ln_post_mix[l])
        h = rms_norm(x, ln_pre_ffn[l])
        f = (jax.nn.silu(h @ w_gate[l]) * (h @ w_up[l])) @ w_down[l]
        x = x + rms_norm(f, ln_post_ffn[l])
    return x
```

```python
import functools

import jax
import jax.numpy as jnp
import numpy as np
from jax import lax
from jax.experimental import pallas as pl
from jax.experimental.pallas import tpu as pltpu

D_MODEL = 1024
POOL_WIDTH = 256
POOL_WINDOWS = (2, 4, 8, 16)
POOL_GROUP = 64
HEAD_DIM = 64
HALF_DIM = HEAD_DIM // 2
ATTN_WIDTH = 768
N_HEADS = 12
DILATED_CONFIGS = ((128, 1), (512, 4), (2048, 16))
BLOCK = 128
ROPE_THETA = 10000.0
IN_WIDTH = POOL_WIDTH + 3 * ATTN_WIDTH
D_FF = 2816
EPS = 1e-6

LANES = 128
MAX_POOL_WIN = max(POOL_WINDOWS)

TOKEN_TILE = 512
ATTN_QUERY_TILE = 512
FF_CHUNK = 256
VMEM_LIMIT = 56 * 1024 * 1024

_BF16 = jnp.bfloat16
_F32 = jnp.float32


def _rms_scale(x):
    return x * lax.rsqrt(jnp.mean(x * x, axis=-1, keepdims=True) + EPS)


def _resident(shape):
    nd = len(shape)
    return pl.BlockSpec(shape, lambda *_: (0,) * nd, pipeline_mode=pl.Buffered(1))


def _rope(t, cos, sin_signed, first_half):
    outs = []
    for c in range(ATTN_WIDTH // LANES):
        xc = t[:, c * LANES:(c + 1) * LANES]
        ahead = pltpu.roll(xc, LANES - HALF_DIM, axis=1)
        behind = pltpu.roll(xc, HALF_DIM, axis=1)
        partner = jnp.where(first_half, ahead, behind)
        outs.append(xc * cos + partner * sin_signed)
    return jnp.concatenate(outs, axis=1)


def _in_proj_kernel(tiles_per_seq, x_ref, g_ref, w_ref, cos_ref, sin_ref, wpool_ref,
                    pscale_ref, pool_ref, q_ref, k_ref, v_ref, halo_ref):
    i = pl.program_id(0)
    tm = x_ref.shape[0]
    x = x_ref[...]
    h = (_rms_scale(x) * g_ref[...]).astype(_BF16)

    u = jnp.dot(h, w_ref[:, 0:POOL_WIDTH], preferred_element_type=_F32)

    @pl.when(i % tiles_per_seq == 0)
    def _():
        halo_ref[...] = jnp.zeros_like(halo_ref)

    ext = jnp.concatenate([halo_ref[...], u], axis=0)
    halo_ref[...] = u[tm - MAX_POOL_WIN:, :]
    s2 = ext[1:, :] + ext[:-1, :]
    s4 = s2[2:, :] + s2[:-2, :]
    s8 = s4[4:, :] + s4[:-4, :]
    s16 = s8[8:, :] + s8[:-8, :]
    w2 = s2[MAX_POOL_WIN - 1:MAX_POOL_WIN - 1 + tm, :]
    w4 = s4[MAX_POOL_WIN - 3:MAX_POOL_WIN - 3 + tm, :]
    w8 = s8[MAX_POOL_WIN - 7:MAX_POOL_WIN - 7 + tm, :]
    w16 = s16[1:1 + tm, :]
    lane = lax.broadcasted_iota(jnp.int32, (tm, POOL_WIDTH), 1)
    grp = lane // POOL_GROUP
    wsum = jnp.where(grp == 0, w2, jnp.where(grp == 1, w4, jnp.where(grp == 2, w8, w16)))
    win = jnp.where(grp == 0, 2, jnp.where(grp == 1, 4, jnp.where(grp == 2, 8, 16)))
    pos = (i % tiles_per_seq) * tm + lax.broadcasted_iota(jnp.int32, (tm, POOL_WIDTH), 0)
    cnt = jnp.minimum(pos + 1, win).astype(_F32)
    d = wsum / cnt - u
    y = jnp.dot(d.astype(_BF16), wpool_ref[...], preferred_element_type=_F32)
    pool_ref[...] = (y * pscale_ref[...]).astype(pool_ref.dtype)

    cos = cos_ref[...]
    sin_signed = sin_ref[...]
    lane128 = lax.broadcasted_iota(jnp.int32, (tm, LANES), 1)
    first_half = (lane128 % HEAD_DIM) < HALF_DIM
    q = jnp.dot(h, w_ref[:, POOL_WIDTH:POOL_WIDTH + ATTN_WIDTH], preferred_element_type=_F32)
    q_ref[...] = (_rope(q, cos, sin_signed, first_half) * (HEAD_DIM ** -0.5)).astype(q_ref.dtype)
    k = jnp.dot(h, w_ref[:, POOL_WIDTH + ATTN_WIDTH:POOL_WIDTH + 2 * ATTN_WIDTH],
                preferred_element_type=_F32)
    k_ref[...] = _rope(k, cos, sin_signed, first_half).astype(k_ref.dtype)
    v = jnp.dot(h, w_ref[:, POOL_WIDTH + 2 * ATTN_WIDTH:], preferred_element_type=_F32)
    v_ref[...] = v.astype(v_ref.dtype)


def _in_proj(x2, g, w_in, cos_t, sin_t, wpool_bd, pscale, seq_len):
    n = x2.shape[0]
    tm = TOKEN_TILE
    tiles_per_seq = seq_len // tm
    row = lambda i: (i, 0)
    tab = lambda i: (i % tiles_per_seq, 0)
    out_shape = (
        jax.ShapeDtypeStruct((n, POOL_WIDTH), _BF16),
        jax.ShapeDtypeStruct((n, ATTN_WIDTH), _BF16),
        jax.ShapeDtypeStruct((n, ATTN_WIDTH), _BF16),
        jax.ShapeDtypeStruct((n, ATTN_WIDTH), _BF16),
    )
    return pl.pallas_call(
        functools.partial(_in_proj_kernel, tiles_per_seq),
        grid=(n // tm,),
        in_specs=[
            pl.BlockSpec((tm, D_MODEL), row),
            _resident((1, D_MODEL)),
            _resident((D_MODEL, IN_WIDTH)),
            pl.BlockSpec((tm, LANES), tab),
            pl.BlockSpec((tm, LANES), tab),
            _resident((POOL_WIDTH, POOL_WIDTH)),
            _resident((1, POOL_WIDTH)),
        ],
        out_specs=(
            pl.BlockSpec((tm, POOL_WIDTH), row),
            pl.BlockSpec((tm, ATTN_WIDTH), row),
            pl.BlockSpec((tm, ATTN_WIDTH), row),
            pl.BlockSpec((tm, ATTN_WIDTH), row),
        ),
        out_shape=out_shape,
        scratch_shapes=[pltpu.VMEM((MAX_POOL_WIN, POOL_WIDTH), _F32)],
        compiler_params=pltpu.CompilerParams(
            dimension_semantics=("arbitrary",), vmem_limit_bytes=VMEM_LIMIT),
        name="in_proj",
    )(x2, g, w_in, cos_t, sin_t, wpool_bd, pscale)


def _attn_kernel(q_ref, kc_ref, kp_ref, vc_ref, vp_ref, o_ref, lse_ref, kbuf, vbuf):
    j = pl.program_id(2)
    n_sub = q_ref.shape[1] // BLOCK
    kbuf[0:BLOCK, :] = kp_ref[0]
    kbuf[BLOCK:, :] = kc_ref[0]
    vbuf[0:BLOCK, :] = vp_ref[0]
    vbuf[BLOCK:, :] = vc_ref[0]

    qi = lax.broadcasted_iota(jnp.int32, (BLOCK, 2 * BLOCK), 0)
    kj = lax.broadcasted_iota(jnp.int32, (BLOCK, 2 * BLOCK), 1)
    dist = qi + BLOCK - kj
    band = (dist >= 0) & (dist <= BLOCK)
    lane = lax.broadcasted_iota(jnp.int32, (BLOCK, LANES), 1)

    def sub_block(s, carry):
        r0 = pl.multiple_of(s * BLOCK, BLOCK)
        has_prev = (j * n_sub + s) > 0
        valid = band & ((kj >= BLOCK) | has_prev)
        lse_tile = jnp.zeros((BLOCK, LANES), _F32)
        for hp in range(N_HEADS // 2):
            halves = []
            for hh in range(2):
                h = 2 * hp + hh
                cols = slice(h * HEAD_DIM, (h + 1) * HEAD_DIM)
                qh = q_ref[0, pl.ds(r0, BLOCK), cols]
                kh = kbuf[pl.ds(r0, 2 * BLOCK), cols]
                vh = vbuf[pl.ds(r0, 2 * BLOCK), cols]
                sc = lax.dot_general(qh, kh, (((1,), (1,)), ((), ())),
                                     preferred_element_type=_F32)
                sc = jnp.where(valid, sc, -jnp.inf)
                m = jnp.max(sc, axis=-1, keepdims=True)
                p = jnp.exp(sc - m)
                den = jnp.sum(p, axis=-1, keepdims=True)
                oh = jnp.dot(p.astype(_BF16), vh, preferred_element_type=_F32)
                halves.append(oh * (1.0 / den))
                lse_tile = jnp.where(lane == h, m + jnp.log(den), lse_tile)
            o_ref[0, pl.ds(r0, BLOCK), hp * LANES:(hp + 1) * LANES] = (
                jnp.concatenate(halves, axis=1).astype(o_ref.dtype))
        lse_ref[0, pl.ds(r0, BLOCK), :] = lse_tile
        return carry

    lax.fori_loop(0, n_sub, sub_block, 0)


def _attention_config(q, k, v, dilation):
    b, s, _ = q.shape
    length = s // dilation
    tq = min(ATTN_QUERY_TILE, length)
    sub_per_tile = tq // BLOCK
    view = lambda a: a.reshape(b, length, dilation * ATTN_WIDTH)
    cur = lambda bi, r, j: (bi, j, r)
    prev = lambda bi, r, j: (bi, jnp.maximum(j * sub_per_tile - 1, 0), r)
    o, lse = pl.pallas_call(
        _attn_kernel,
        grid=(b, dilation, length // tq),
        in_specs=[
            pl.BlockSpec((1, tq, ATTN_WIDTH), cur),
            pl.BlockSpec((1, tq, ATTN_WIDTH), cur),
            pl.BlockSpec((1, BLOCK, ATTN_WIDTH), prev),
            pl.BlockSpec((1, tq, ATTN_WIDTH), cur),
            pl.BlockSpec((1, BLOCK, ATTN_WIDTH), prev),
        ],
        out_specs=(
            pl.BlockSpec((1, tq, ATTN_WIDTH), cur),
            pl.BlockSpec((1, tq, LANES), cur),
        ),
        out_shape=(
            jax.ShapeDtypeStruct((b, length, dilation * ATTN_WIDTH), _BF16),
            jax.ShapeDtypeStruct((b, length, dilation * LANES), _F32),
        ),
        scratch_shapes=[
            pltpu.VMEM((BLOCK + tq, ATTN_WIDTH), _BF16),
            pltpu.VMEM((BLOCK + tq, ATTN_WIDTH), _BF16),
        ],
        compiler_params=pltpu.CompilerParams(
            dimension_semantics=("arbitrary", "arbitrary", "arbitrary"),
            vmem_limit_bytes=VMEM_LIMIT),
        name=f"attn_d{dilation}",
    )(view(q), view(k), view(k), view(v), view(v))
    return o.reshape(b * s, ATTN_WIDTH), lse.reshape(b * s, LANES)


def _mix_out_kernel(x_ref, pool_ref, o1_ref, o2_ref, o3_ref, l1_ref, l2_ref, l3_ref,
                    expand_ref, w_ref, g_ref, out_ref):
    l1, l2, l3 = l1_ref[...], l2_ref[...], l3_ref[...]
    mx = jnp.maximum(jnp.maximum(l1, l2), l3)
    e1, e2, e3 = jnp.exp(l1 - mx), jnp.exp(l2 - mx), jnp.exp(l3 - mx)
    inv = 1.0 / (e1 + e2 + e3)
    attn = None
    for e, o_ref in ((e1, o1_ref), (e2, o2_ref), (e3, o3_ref)):
        wexp = jnp.dot((e * inv).astype(_BF16), expand_ref[...], preferred_element_type=_F32)
        term = wexp * o_ref[...].astype(_F32)
        attn = term if attn is None else attn + term
    mix = jnp.dot(pool_ref[...], w_ref[0:POOL_WIDTH, :], preferred_element_type=_F32)
    mix = mix + jnp.dot(attn.astype(_BF16), w_ref[POOL_WIDTH:, :], preferred_element_type=_F32)
    out_ref[...] = x_ref[...] + _rms_scale(mix) * g_ref[...]


def _mix_out(x2, pool, outs, lses, expand, w_out, g):
    n = x2.shape[0]
    tm = TOKEN_TILE
    row = lambda i: (i, 0)
    return pl.pallas_call(
        _mix_out_kernel,
        grid=(n // tm,),
        in_specs=[
            pl.BlockSpec((tm, D_MODEL), row),
            pl.BlockSpec((tm, POOL_WIDTH), row),
            pl.BlockSpec((tm, ATTN_WIDTH), row),
            pl.BlockSpec((tm, ATTN_WIDTH), row),
            pl.BlockSpec((tm, ATTN_WIDTH), row),
            pl.BlockSpec((tm, LANES), row),
            pl.BlockSpec((tm, LANES), row),
            pl.BlockSpec((tm, LANES), row),
            _resident((LANES, ATTN_WIDTH)),
            _resident((D_MODEL, D_MODEL)),
            _resident((1, D_MODEL)),
        ],
        out_specs=pl.BlockSpec((tm, D_MODEL), row),
        out_shape=jax.ShapeDtypeStruct((n, D_MODEL), _F32),
        compiler_params=pltpu.CompilerParams(
            dimension_semantics=("arbitrary",), vmem_limit_bytes=VMEM_LIMIT),
        name="mix_out",
    )(x2, pool, *outs, *lses, expand, w_out, g)


def _ffn_kernel(x_ref, gpre_ref, wg_ref, wu_ref, wd_ref, gpost_ref, out_ref, act_ref):
    x = x_ref[...]
    h = (_rms_scale(x) * gpre_ref[...]).astype(_BF16)
    for c in range(D_FF // FF_CHUNK):
        cols = slice(c * FF_CHUNK, (c + 1) * FF_CHUNK)
        gate = jnp.dot(h, wg_ref[:, cols], preferred_element_type=_F32)
        up = jnp.dot(h, wu_ref[:, cols], preferred_element_type=_F32)
        act_ref[:, cols] = (gate * jax.nn.sigmoid(gate) * up).astype(act_ref.dtype)
    f = jnp.dot(act_ref[...], wd_ref[...], preferred_element_type=_F32)
    out_ref[...] = x + _rms_scale(f) * gpost_ref[...]


def _ffn(x2, gpre, wg, wu, wd, gpost):
    n = x2.shape[0]
    tm = TOKEN_TILE
    row = lambda i: (i, 0)
    return pl.pallas_call(
        _ffn_kernel,
        grid=(n // tm,),
        in_specs=[
            pl.BlockSpec((tm, D_MODEL), row),
            _resident((1, D_MODEL)),
            _resident((D_MODEL, D_FF)),
            _resident((D_MODEL, D_FF)),
            _resident((D_FF, D_MODEL)),
            _resident((1, D_MODEL)),
        ],
        out_specs=pl.BlockSpec((tm, D_MODEL), row),
        out_shape=jax.ShapeDtypeStruct((n, D_MODEL), _F32),
        scratch_shapes=[pltpu.VMEM((tm, D_FF), _BF16)],
        compiler_params=pltpu.CompilerParams(
            dimension_semantics=("arbitrary",), vmem_limit_bytes=VMEM_LIMIT),
        name="ffn",
    )(x2, gpre, wg, wu, wd, gpost)


def _rope_tables(seq_len):
    freqs = ROPE_THETA ** (-jnp.arange(HALF_DIM, dtype=_F32) * (2.0 / HEAD_DIM))
    ang = jnp.arange(seq_len).astype(_F32)[:, None] * freqs[None, :]
    cos, sin = jnp.cos(ang), jnp.sin(ang)
    cos_t = jnp.concatenate([cos, cos, cos, cos], axis=1)
    sin_t = jnp.concatenate([-sin, sin, -sin, sin], axis=1)
    return cos_t, sin_t


def _block_diag(w_pool):
    n_g = w_pool.shape[0]
    out = jnp.zeros((POOL_WIDTH, POOL_WIDTH), w_pool.dtype)
    for gi in range(n_g):
        out = out.at[gi * POOL_GROUP:(gi + 1) * POOL_GROUP,
                     gi * POOL_GROUP:(gi + 1) * POOL_GROUP].set(w_pool[gi])
    return out


def _head_expand():
    e = np.zeros((LANES, ATTN_WIDTH), np.float32)
    for h in range(N_HEADS):
        e[h, h * HEAD_DIM:(h + 1) * HEAD_DIM] = 1.0
    return jnp.asarray(e, dtype=_BF16)


def kernel(x, ln_pre_mix, w_in, w_pool, pool_scale, w_out, ln_post_mix, ln_pre_ffn,
           w_gate, w_up, w_down, ln_post_ffn):
    b, s, _ = x.shape
    depth = w_in.shape[0]
    cos_t, sin_t = _rope_tables(s)
    expand = _head_expand()
    x2 = x.reshape(b * s, D_MODEL)
    for l in range(depth):
        pool, q, k, v = _in_proj(
            x2, ln_pre_mix[l][None, :], w_in[l].astype(_BF16), cos_t, sin_t,
            _block_diag(w_pool[l]).astype(_BF16), pool_scale[l][None, :], s)
        q3, k3, v3 = (a.reshape(b, s, ATTN_WIDTH) for a in (q, k, v))
        outs, lses = [], []
        for _, dilation in DILATED_CONFIGS:
            o, lse = _attention_config(q3, k3, v3, dilation)
            outs.append(o)
            lses.append(lse)
        x2 = _mix_out(x2, pool, outs, lses, expand, w_out[l].astype(_BF16),
                      ln_post_mix[l][None, :])
        x2 = _ffn(x2, ln_pre_ffn[l][None, :], w_gate[l].astype(_BF16), w_up[l].astype(_BF16),
                  w_down[l].astype(_BF16), ln_post_ffn[l][None, :])
    return x2.reshape(b, s, D_MODEL)
```

```python
import functools
import math

import jax
import jax.numpy as jnp
import numpy as np
from jax import lax
from jax.experimental import pallas as pl
from jax.experimental.pallas import tpu as pltpu

D_MODEL = 1024
POOL_WIDTH = 256
POOL_WINDOWS = (2, 4, 8, 16)
POOL_GROUP = 64
HEAD_DIM = 64
HALF_DIM = HEAD_DIM // 2
ATTN_WIDTH = 768
N_HEADS = 12
DILATED_CONFIGS = ((128, 1), (512, 4), (2048, 16))
BLOCK = 128
ROPE_THETA = 10000.0
IN_WIDTH = POOL_WIDTH + 3 * ATTN_WIDTH
D_FF = 2816
EPS = 1e-6

LANES = 128
MAX_POOL_WIN = max(POOL_WINDOWS)

N_PAIRS = ATTN_WIDTH // LANES
HEADS_PER_PAIR = LANES // HEAD_DIM

TOKEN_TILE = 512
ATTN_CHUNK = 2048
PAIR_GROUPS = 2
PAIRS_PER_GROUP = N_PAIRS // PAIR_GROUPS
HEADS_PER_GROUP = PAIRS_PER_GROUP * HEADS_PER_PAIR
FF_CHUNK = 256
VMEM_LIMIT = 56 * 1024 * 1024

LOG2E = math.log2(math.e)
LN2 = math.log(2.0)

_BF16 = jnp.bfloat16
_F32 = jnp.float32


def _rms_scale(x):
    return x * lax.rsqrt(jnp.mean(x * x, axis=-1, keepdims=True) + EPS)


def _resident(shape):
    nd = len(shape)
    return pl.BlockSpec(shape, lambda *_: (0,) * nd, pipeline_mode=pl.Buffered(1))


def _rope_slab(xc, cos, sin_signed, first_half):
    ahead = pltpu.roll(xc, LANES - HALF_DIM, axis=1)
    behind = pltpu.roll(xc, HALF_DIM, axis=1)
    partner = jnp.where(first_half, ahead, behind)
    return xc * cos + partner * sin_signed


def _in_proj_kernel(x_ref, g_ref, w_ref, cos_ref, sin_ref, wpool_ref, pscale_ref,
                    pool_ref, q_ref, k_ref, v_ref, halo_ref):
    i = pl.program_id(1)
    tm = x_ref.shape[1]
    x = x_ref[0]
    h = (_rms_scale(x) * g_ref[...]).astype(_BF16)

    u = jnp.dot(h, w_ref[:, 0:POOL_WIDTH], preferred_element_type=_F32)

    @pl.when(i == 0)
    def _():
        halo_ref[...] = jnp.zeros_like(halo_ref)

    ext = jnp.concatenate([halo_ref[...], u], axis=0)
    halo_ref[...] = u[tm - MAX_POOL_WIN:, :]
    s2 = ext[1:, :] + ext[:-1, :]
    s4 = s2[2:, :] + s2[:-2, :]
    s8 = s4[4:, :] + s4[:-4, :]
    s16 = s8[8:, :] + s8[:-8, :]
    w2 = s2[MAX_POOL_WIN - 1:MAX_POOL_WIN - 1 + tm, :]
    w4 = s4[MAX_POOL_WIN - 3:MAX_POOL_WIN - 3 + tm, :]
    w8 = s8[MAX_POOL_WIN - 7:MAX_POOL_WIN - 7 + tm, :]
    w16 = s16[1:1 + tm, :]
    lane = lax.broadcasted_iota(jnp.int32, (tm, POOL_WIDTH), 1)
    grp = lane // POOL_GROUP
    wsum = jnp.where(grp == 0, w2, jnp.where(grp == 1, w4, jnp.where(grp == 2, w8, w16)))
    win = jnp.where(grp == 0, 2, jnp.where(grp == 1, 4, jnp.where(grp == 2, 8, 16)))
    pos = i * tm + lax.broadcasted_iota(jnp.int32, (tm, POOL_WIDTH), 0)
    cnt = jnp.minimum(pos + 1, win).astype(_F32)
    d = wsum / cnt - u
    y = jnp.dot(d.astype(_BF16), wpool_ref[...], preferred_element_type=_F32)
    pool_ref[0] = (y * pscale_ref[...]).astype(pool_ref.dtype)

    cos = cos_ref[...]
    sin_signed = sin_ref[...]
    lane128 = lax.broadcasted_iota(jnp.int32, (tm, LANES), 1)
    first_half = (lane128 % HEAD_DIM) < HALF_DIM
    q_scale = (HEAD_DIM ** -0.5) * LOG2E
    q0, k0, v0 = POOL_WIDTH, POOL_WIDTH + ATTN_WIDTH, POOL_WIDTH + 2 * ATTN_WIDTH
    q = jnp.dot(h, w_ref[:, q0:k0], preferred_element_type=_F32)
    for p in range(N_PAIRS):
        q_ref[0, p] = _rope_slab(q[:, p * LANES:(p + 1) * LANES], cos, sin_signed, first_half) * q_scale
    k = jnp.dot(h, w_ref[:, k0:v0], preferred_element_type=_F32)
    for p in range(N_PAIRS):
        k_ref[0, p] = _rope_slab(k[:, p * LANES:(p + 1) * LANES], cos, sin_signed, first_half)
    v = jnp.dot(h, w_ref[:, v0:], preferred_element_type=_F32)
    for p in range(N_PAIRS):
        v_ref[0, p] = v[:, p * LANES:(p + 1) * LANES]


def _in_proj(x, g, w_in, cos_t, sin_t, wpool_bd, pscale):
    b, s, _ = x.shape
    tm = TOKEN_TILE
    row = lambda bi, i: (bi, i, 0)
    tab = lambda bi, i: (i, 0)
    slab = lambda bi, i: (bi, 0, i, 0)
    slab_shape = jax.ShapeDtypeStruct((b, N_PAIRS, s, LANES), _F32)
    return pl.pallas_call(
        _in_proj_kernel,
        grid=(b, s // tm),
        in_specs=[
            pl.BlockSpec((1, tm, D_MODEL), row),
            _resident((1, D_MODEL)),
            _resident((D_MODEL, IN_WIDTH)),
            pl.BlockSpec((tm, LANES), tab),
            pl.BlockSpec((tm, LANES), tab),
            _resident((POOL_WIDTH, POOL_WIDTH)),
            _resident((1, POOL_WIDTH)),
        ],
        out_specs=(
            pl.BlockSpec((1, tm, POOL_WIDTH), row),
            pl.BlockSpec((1, N_PAIRS, tm, LANES), slab),
            pl.BlockSpec((1, N_PAIRS, tm, LANES), slab),
            pl.BlockSpec((1, N_PAIRS, tm, LANES), slab),
        ),
        out_shape=(jax.ShapeDtypeStruct((b, s, POOL_WIDTH), _BF16), slab_shape, slab_shape, slab_shape),
        scratch_shapes=[pltpu.VMEM((MAX_POOL_WIN, POOL_WIDTH), _F32)],
        compiler_params=pltpu.CompilerParams(
            dimension_semantics=("arbitrary", "arbitrary"), vmem_limit_bytes=VMEM_LIMIT),
        name="in_proj",
    )(x, g, w_in, cos_t, sin_t, wpool_bd, pscale)


def _attn_kernel(dilation, q_ref, k_ref, v_ref, o_ref, lse_ref, kcur, vcur, kprev, vprev):
    c = pl.program_id(2)
    chunk = q_ref.shape[2]
    n_pairs = q_ref.shape[1]
    per_res = chunk // (dilation * BLOCK)

    def rows(r, i):
        start = r + dilation * BLOCK * i
        if dilation == 1:
            return pl.ds(start, BLOCK)
        return pl.ds(start, BLOCK, stride=dilation)

    def blk_rows(blk):
        return slice(blk * BLOCK, (blk + 1) * BLOCK)

    @pl.when(c == 0)
    def _():
        kprev[...] = jnp.zeros_like(kprev)
        vprev[...] = jnp.zeros_like(vprev)

    for r in range(dilation):
        for i in range(per_res):
            blk = r * per_res + i
            for p in range(n_pairs):
                kcur[p, blk_rows(blk), :] = k_ref[0, p, rows(r, i), :].astype(_BF16)
                vcur[p, blk_rows(blk), :] = v_ref[0, p, rows(r, i), :].astype(_BF16)

    qi = lax.broadcasted_iota(jnp.int32, (BLOCK, 2 * BLOCK), 0)
    kj = lax.broadcasted_iota(jnp.int32, (BLOCK, 2 * BLOCK), 1)
    dist = qi + BLOCK - kj
    band = (dist >= 0) & (dist <= BLOCK)
    band_first = band & (kj >= BLOCK * (1 - jnp.minimum(c, 1)))
    lane = lax.broadcasted_iota(jnp.int32, (BLOCK, LANES), 1)
    low_head = lane < HEAD_DIM
    ones_block = jnp.ones((2 * BLOCK, LANES), _BF16)
    zero_bf16 = jnp.zeros((BLOCK, LANES), _BF16)

    for r in range(dilation):
        for i in range(per_res):
            blk = r * per_res + i
            valid = band_first if i == 0 else band
            lse_tile = jnp.zeros((BLOCK, LANES), _F32)
            for p in range(n_pairs):
                q = q_ref[0, p, rows(r, i), :].astype(_BF16)
                q2 = jnp.concatenate([jnp.where(low_head, q, zero_bf16),
                                      jnp.where(low_head, zero_bf16, q)], axis=0)
                if i == 0:
                    k_prev, v_prev = kprev[p, blk_rows(r), :], vprev[p, blk_rows(r), :]
                else:
                    k_prev, v_prev = kcur[p, blk_rows(blk - 1), :], vcur[p, blk_rows(blk - 1), :]
                k_band = jnp.concatenate([k_prev, kcur[p, blk_rows(blk), :]], axis=0)
                v_band = jnp.concatenate([v_prev, vcur[p, blk_rows(blk), :]], axis=0)
                sc = lax.dot_general(q2, k_band, (((1,), (1,)), ((), ())),
                                     preferred_element_type=_F32)
                halves = []
                for hh in range(HEADS_PER_PAIR):
                    s_h = jnp.where(valid, sc[hh * BLOCK:(hh + 1) * BLOCK, :], -jnp.inf)
                    m = jnp.max(s_h, axis=-1, keepdims=True)
                    halves.append((m, jnp.exp2(s_h - m).astype(_BF16)))
                prob = jnp.concatenate([halves[0][1], halves[1][1]], axis=0)
                v_aug = jnp.concatenate([v_band, ones_block], axis=1)
                pv = jnp.dot(prob, v_aug, preferred_element_type=_F32)
                den_lo, den_hi = pv[0:BLOCK, LANES:], pv[BLOCK:, LANES:]
                out = jnp.where(low_head, pv[0:BLOCK, 0:LANES] * (1.0 / den_lo),
                                pv[BLOCK:, 0:LANES] * (1.0 / den_hi))
                o_ref[0, p, rows(r, i), :] = out
                lse_lo = (halves[0][0] + jnp.log2(den_lo)) * LN2
                lse_hi = (halves[1][0] + jnp.log2(den_hi)) * LN2
                lse_tile = jnp.where(lane == HEADS_PER_PAIR * p, lse_lo,
                                     jnp.where(lane == HEADS_PER_PAIR * p + 1, lse_hi, lse_tile))
            lse_ref[0, 0, rows(r, i), :] = lse_tile

    for r in range(dilation):
        last = r * per_res + per_res - 1
        for p in range(n_pairs):
            kprev[p, blk_rows(r), :] = kcur[p, blk_rows(last), :]
            vprev[p, blk_rows(r), :] = vcur[p, blk_rows(last), :]


def _attention_config(q, k, v, dilation):
    b, _, s, _ = q.shape
    chunk = ATTN_CHUNK
    pg = PAIRS_PER_GROUP
    slab = lambda bi, g, c: (bi, g, c, 0)
    return pl.pallas_call(
        functools.partial(_attn_kernel, dilation),
        grid=(b, PAIR_GROUPS, s // chunk),
        in_specs=[pl.BlockSpec((1, pg, chunk, LANES), slab)] * 3,
        out_specs=(
            pl.BlockSpec((1, pg, chunk, LANES), slab),
            pl.BlockSpec((1, 1, chunk, LANES), slab),
        ),
        out_shape=(
            jax.ShapeDtypeStruct((b, N_PAIRS, s, LANES), _F32),
            jax.ShapeDtypeStruct((b, PAIR_GROUPS, s, LANES), _F32),
        ),
        scratch_shapes=[
            pltpu.VMEM((pg, chunk, LANES), _BF16),
            pltpu.VMEM((pg, chunk, LANES), _BF16),
            pltpu.VMEM((pg, dilation * BLOCK, LANES), _BF16),
            pltpu.VMEM((pg, dilation * BLOCK, LANES), _BF16),
        ],
        compiler_params=pltpu.CompilerParams(
            dimension_semantics=("arbitrary", "arbitrary", "arbitrary"),
            vmem_limit_bytes=VMEM_LIMIT),
        name=f"attn_d{dilation}",
    )(q, k, v)


def _mix_out_kernel(x_ref, pool_ref, o1_ref, o2_ref, o3_ref, l1_ref, l2_ref, l3_ref,
                    expand_ref, w_ref, g_ref, out_ref):
    tm = x_ref.shape[1]
    lane = lax.broadcasted_iota(jnp.int32, (tm, LANES), 1)

    def head_lse(l_ref):
        return jnp.where(lane < HEADS_PER_GROUP, l_ref[0, 0],
                         pltpu.roll(l_ref[0, 1], HEADS_PER_GROUP, axis=1))

    l1, l2, l3 = head_lse(l1_ref), head_lse(l2_ref), head_lse(l3_ref)
    mx = jnp.maximum(jnp.maximum(l1, l2), l3)
    e1, e2, e3 = jnp.exp(l1 - mx), jnp.exp(l2 - mx), jnp.exp(l3 - mx)
    inv = 1.0 / (e1 + e2 + e3)
    attn = None
    for e, o_ref in ((e1, o1_ref), (e2, o2_ref), (e3, o3_ref)):
        wexp = jnp.dot((e * inv).astype(_BF16), expand_ref[...], preferred_element_type=_F32)
        o = jnp.concatenate([o_ref[0, p] for p in range(N_PAIRS)], axis=1)
        term = wexp * o
        attn = term if attn is None else attn + term
    mix = jnp.dot(pool_ref[0], w_ref[0:POOL_WIDTH, :], preferred_element_type=_F32)
    mix = mix + jnp.dot(attn.astype(_BF16), w_ref[POOL_WIDTH:, :], preferred_element_type=_F32)
    out_ref[0] = x_ref[0] + _rms_scale(mix) * g_ref[...]


def _mix_out(x, pool, outs, lses, expand, w_out, g):
    b, s, _ = x.shape
    tm = TOKEN_TILE
    row = lambda bi, i: (bi, i, 0)
    slab = lambda bi, i: (bi, 0, i, 0)
    return pl.pallas_call(
        _mix_out_kernel,
        grid=(b, s // tm),
        in_specs=[
            pl.BlockSpec((1, tm, D_MODEL), row),
            pl.BlockSpec((1, tm, POOL_WIDTH), row),
            pl.BlockSpec((1, N_PAIRS, tm, LANES), slab),
            pl.BlockSpec((1, N_PAIRS, tm, LANES), slab),
            pl.BlockSpec((1, N_PAIRS, tm, LANES), slab),
            pl.BlockSpec((1, PAIR_GROUPS, tm, LANES), slab),
            pl.BlockSpec((1, PAIR_GROUPS, tm, LANES), slab),
            pl.BlockSpec((1, PAIR_GROUPS, tm, LANES), slab),
            _resident((LANES, ATTN_WIDTH)),
            _resident((D_MODEL, D_MODEL)),
            _resident((1, D_MODEL)),
        ],
        out_specs=pl.BlockSpec((1, tm, D_MODEL), row),
        out_shape=jax.ShapeDtypeStruct((b, s, D_MODEL), _F32),
        compiler_params=pltpu.CompilerParams(
            dimension_semantics=("arbitrary", "arbitrary"), vmem_limit_bytes=VMEM_LIMIT),
        name="mix_out",
    )(x, pool, *outs, *lses, expand, w_out, g)


def _ffn_kernel(x_ref, gpre_ref, wg_ref, wu_ref, wd_ref, gpost_ref, out_ref, act_ref):
    x = x_ref[...]
    h = (_rms_scale(x) * gpre_ref[...]).astype(_BF16)
    for c in range(D_FF // FF_CHUNK):
        cols = slice(c * FF_CHUNK, (c + 1) * FF_CHUNK)
        gate = jnp.dot(h, wg_ref[:, cols], preferred_element_type=_F32)
        up = jnp.dot(h, wu_ref[:, cols], preferred_element_type=_F32)
        act_ref[:, cols] = (gate * jax.nn.sigmoid(gate) * up).astype(act_ref.dtype)
    f = jnp.dot(act_ref[...], wd_ref[...], preferred_element_type=_F32)
    out_ref[...] = x + _rms_scale(f) * gpost_ref[...]


def _ffn(x2, gpre, wg, wu, wd, gpost):
    n = x2.shape[0]
    tm = TOKEN_TILE
    row = lambda i: (i, 0)
    return pl.pallas_call(
        _ffn_kernel,
        grid=(n // tm,),
        in_specs=[
            pl.BlockSpec((tm, D_MODEL), row),
            _resident((1, D_MODEL)),
            _resident((D_MODEL, D_FF)),
            _resident((D_MODEL, D_FF)),
            _resident((D_FF, D_MODEL)),
            _resident((1, D_MODEL)),
        ],
        out_specs=pl.BlockSpec((tm, D_MODEL), row),
        out_shape=jax.ShapeDtypeStruct((n, D_MODEL), _F32),
        scratch_shapes=[pltpu.VMEM((tm, D_FF), _BF16)],
        compiler_params=pltpu.CompilerParams(
            dimension_semantics=("arbitrary",), vmem_limit_bytes=VMEM_LIMIT),
        name="ffn",
    )(x2, gpre, wg, wu, wd, gpost)


def _rope_tables(seq_len):
    freqs = ROPE_THETA ** (-jnp.arange(HALF_DIM, dtype=_F32) * (2.0 / HEAD_DIM))
    ang = jnp.arange(seq_len).astype(_F32)[:, None] * freqs[None, :]
    cos, sin = jnp.cos(ang), jnp.sin(ang)
    cos_t = jnp.concatenate([cos, cos, cos, cos], axis=1)
    sin_t = jnp.concatenate([-sin, sin, -sin, sin], axis=1)
    return cos_t, sin_t


def _block_diag(w_pool):
    n_g = w_pool.shape[0]
    out = jnp.zeros((POOL_WIDTH, POOL_WIDTH), w_pool.dtype)
    for gi in range(n_g):
        out = out.at[gi * POOL_GROUP:(gi + 1) * POOL_GROUP,
                     gi * POOL_GROUP:(gi + 1) * POOL_GROUP].set(w_pool[gi])
    return out


def _head_expand():
    e = np.zeros((LANES, ATTN_WIDTH), np.float32)
    for h in range(N_HEADS):
        e[h, h * HEAD_DIM:(h + 1) * HEAD_DIM] = 1.0
    return jnp.asarray(e, dtype=_BF16)


def kernel(x, ln_pre_mix, w_in, w_pool, pool_scale, w_out, ln_post_mix, ln_pre_ffn,
           w_gate, w_up, w_down, ln_post_ffn):
    b, s, _ = x.shape
    depth = w_in.shape[0]
    cos_t, sin_t = _rope_tables(s)
    expand = _head_expand()
    for l in range(depth):
        pool, q, k, v = _in_proj(
            x, ln_pre_mix[l][None, :], w_in[l].astype(_BF16), cos_t, sin_t,
            _block_diag(w_pool[l]).astype(_BF16), pool_scale[l][None, :])
        outs, lses = [], []
        for _, dilation in DILATED_CONFIGS:
            o, lse = _attention_config(q, k, v, dilation)
            outs.append(o)
            lses.append(lse)
        x = _mix_out(x, pool, outs, lses, expand, w_out[l].astype(_BF16), ln_post_mix[l][None, :])
        x = _ffn(x.reshape(b * s, D_MODEL), ln_pre_ffn[l][None, :], w_gate[l].astype(_BF16),
                 w_up[l].astype(_BF16), w_down[l].astype(_BF16),
                 ln_post_ffn[l][None, :]).reshape(b, s, D_MODEL)
    return x
```

```python
import functools
import math

import jax
import jax.numpy as jnp
import numpy as np
from jax import lax
from jax.experimental import pallas as pl
from jax.experimental.pallas import tpu as pltpu

D_MODEL = 1024
POOL_WIDTH = 256
POOL_WINDOWS = (2, 4, 8, 16)
POOL_GROUP = 64
HEAD_DIM = 64
HALF_DIM = HEAD_DIM // 2
ATTN_WIDTH = 768
N_HEADS = 12
DILATED_CONFIGS = ((128, 1), (512, 4), (2048, 16))
BLOCK = 128
ROPE_THETA = 10000.0
IN_WIDTH = POOL_WIDTH + 3 * ATTN_WIDTH
D_FF = 2816
EPS = 1e-6

LANES = 128
MAX_POOL_WIN = max(POOL_WINDOWS)

N_PAIRS = ATTN_WIDTH // LANES
HEADS_PER_PAIR = LANES // HEAD_DIM

TOKEN_TILE = 512
ATTN_CHUNK = 2048
PAIR_GROUPS = 2
PAIRS_PER_GROUP = N_PAIRS // PAIR_GROUPS
GATHER_STRIDE = 4
FF_CHUNK = 256
VMEM_LIMIT = 56 * 1024 * 1024

LOG2E = math.log2(math.e)
LN2 = math.log(2.0)

_BF16 = jnp.bfloat16
_F32 = jnp.float32


def _rms_scale(x):
    return x * lax.rsqrt(jnp.mean(x * x, axis=-1, keepdims=True) + EPS)


def _resident(shape):
    nd = len(shape)
    return pl.BlockSpec(shape, lambda *_: (0,) * nd, pipeline_mode=pl.Buffered(1))


def _rope_slab(xc, cos, sin_signed, first_half):
    ahead = pltpu.roll(xc, LANES - HALF_DIM, axis=1)
    behind = pltpu.roll(xc, HALF_DIM, axis=1)
    partner = jnp.where(first_half, ahead, behind)
    return xc * cos + partner * sin_signed


def _in_proj_kernel(x_ref, g_ref, w_ref, cos_ref, sin_ref, wpool_ref, pscale_ref,
                    pool_ref, q_ref, k_ref, v_ref, halo_ref):
    i = pl.program_id(1)
    tm = x_ref.shape[1]
    x = x_ref[0]
    h = (_rms_scale(x) * g_ref[...]).astype(_BF16)

    u = jnp.dot(h, w_ref[:, 0:POOL_WIDTH], preferred_element_type=_F32)

    @pl.when(i == 0)
    def _():
        halo_ref[...] = jnp.zeros_like(halo_ref)

    ext = jnp.concatenate([halo_ref[...], u], axis=0)
    halo_ref[...] = u[tm - MAX_POOL_WIN:, :]
    s2 = ext[1:, :] + ext[:-1, :]
    s4 = s2[2:, :] + s2[:-2, :]
    s8 = s4[4:, :] + s4[:-4, :]
    s16 = s8[8:, :] + s8[:-8, :]
    w2 = s2[MAX_POOL_WIN - 1:MAX_POOL_WIN - 1 + tm, :]
    w4 = s4[MAX_POOL_WIN - 3:MAX_POOL_WIN - 3 + tm, :]
    w8 = s8[MAX_POOL_WIN - 7:MAX_POOL_WIN - 7 + tm, :]
    w16 = s16[1:1 + tm, :]
    lane = lax.broadcasted_iota(jnp.int32, (tm, POOL_WIDTH), 1)
    grp = lane // POOL_GROUP
    wsum = jnp.where(grp == 0, w2, jnp.where(grp == 1, w4, jnp.where(grp == 2, w8, w16)))
    win = jnp.where(grp == 0, 2, jnp.where(grp == 1, 4, jnp.where(grp == 2, 8, 16)))
    pos = i * tm + lax.broadcasted_iota(jnp.int32, (tm, POOL_WIDTH), 0)
    cnt = jnp.minimum(pos + 1, win).astype(_F32)
    d = wsum / cnt - u
    y = jnp.dot(d.astype(_BF16), wpool_ref[...], preferred_element_type=_F32)
    pool_ref[0] = (y * pscale_ref[...]).astype(pool_ref.dtype)

    cos = cos_ref[...]
    sin_signed = sin_ref[...]
    lane128 = lax.broadcasted_iota(jnp.int32, (tm, LANES), 1)
    first_half = (lane128 % HEAD_DIM) < HALF_DIM
    q_scale = (HEAD_DIM ** -0.5) * LOG2E
    q0, k0, v0 = POOL_WIDTH, POOL_WIDTH + ATTN_WIDTH, POOL_WIDTH + 2 * ATTN_WIDTH
    q = jnp.dot(h, w_ref[:, q0:k0], preferred_element_type=_F32)
    for p in range(N_PAIRS):
        q_ref[0, p] = (_rope_slab(q[:, p * LANES:(p + 1) * LANES], cos, sin_signed, first_half)
                       * q_scale).astype(q_ref.dtype)
    k = jnp.dot(h, w_ref[:, k0:v0], preferred_element_type=_F32)
    for p in range(N_PAIRS):
        k_ref[0, p] = _rope_slab(k[:, p * LANES:(p + 1) * LANES], cos, sin_signed,
                                 first_half).astype(k_ref.dtype)
    v = jnp.dot(h, w_ref[:, v0:], preferred_element_type=_F32)
    for p in range(N_PAIRS):
        v_ref[0, p] = v[:, p * LANES:(p + 1) * LANES].astype(v_ref.dtype)


def _in_proj(x, g, w_in, cos_t, sin_t, wpool_bd, pscale):
    b, s, _ = x.shape
    tm = TOKEN_TILE
    row = lambda bi, i: (bi, i, 0)
    tab = lambda bi, i: (i, 0)
    slab = lambda bi, i: (bi, 0, i, 0)
    slab_shape = jax.ShapeDtypeStruct((b, N_PAIRS, s, LANES), _BF16)
    return pl.pallas_call(
        _in_proj_kernel,
        grid=(b, s // tm),
        in_specs=[
            pl.BlockSpec((1, tm, D_MODEL), row),
            _resident((1, D_MODEL)),
            _resident((D_MODEL, IN_WIDTH)),
            pl.BlockSpec((tm, LANES), tab),
            pl.BlockSpec((tm, LANES), tab),
            _resident((POOL_WIDTH, POOL_WIDTH)),
            _resident((1, POOL_WIDTH)),
        ],
        out_specs=(
            pl.BlockSpec((1, tm, POOL_WIDTH), row),
            pl.BlockSpec((1, N_PAIRS, tm, LANES), slab),
            pl.BlockSpec((1, N_PAIRS, tm, LANES), slab),
            pl.BlockSpec((1, N_PAIRS, tm, LANES), slab),
        ),
        out_shape=(jax.ShapeDtypeStruct((b, s, POOL_WIDTH), _BF16), slab_shape, slab_shape, slab_shape),
        scratch_shapes=[pltpu.VMEM((MAX_POOL_WIN, POOL_WIDTH), _F32)],
        compiler_params=pltpu.CompilerParams(
            dimension_semantics=("arbitrary", "arbitrary"), vmem_limit_bytes=VMEM_LIMIT),
        name="in_proj",
    )(x, g, w_in, cos_t, sin_t, wpool_bd, pscale)


def _blk_rows(blk):
    return slice(blk * BLOCK, (blk + 1) * BLOCK)


def _gather_streams(dilation, src, dst, st_a, st_b):
    chunk = src.shape[0]
    st_a[...] = src[...].astype(_F32)
    if dilation == GATHER_STRIDE:
        per_res = chunk // dilation
        for r in range(dilation):
            dst[r * per_res:(r + 1) * per_res, :] = (
                st_a[pl.ds(r, per_res, stride=GATHER_STRIDE), :].astype(dst.dtype))
    else:
        quarter = chunk // GATHER_STRIDE
        per_res = chunk // dilation
        for a in range(GATHER_STRIDE):
            st_b[a * quarter:(a + 1) * quarter, :] = st_a[pl.ds(a, quarter, stride=GATHER_STRIDE), :]
        for r in range(dilation):
            a, b = r % GATHER_STRIDE, r // GATHER_STRIDE
            dst[r * per_res:(r + 1) * per_res, :] = (
                st_b[pl.ds(a * quarter + b, per_res, stride=GATHER_STRIDE), :].astype(dst.dtype))


def _scatter_streams(dilation, src, dst, st_a, st_b):
    chunk = src.shape[0]
    if dilation == GATHER_STRIDE:
        per_res = chunk // dilation
        for r in range(dilation):
            st_a[pl.ds(r, per_res, stride=GATHER_STRIDE), :] = src[r * per_res:(r + 1) * per_res, :]
    else:
        quarter = chunk // GATHER_STRIDE
        per_res = chunk // dilation
        for r in range(dilation):
            a, b = r % GATHER_STRIDE, r // GATHER_STRIDE
            st_b[pl.ds(a * quarter + b, per_res, stride=GATHER_STRIDE), :] = (
                src[r * per_res:(r + 1) * per_res, :])
        for a in range(GATHER_STRIDE):
            st_a[pl.ds(a, quarter, stride=GATHER_STRIDE), :] = st_b[a * quarter:(a + 1) * quarter, :]
    dst[...] = st_a[...].astype(dst.dtype)


def _attn_kernel(dilation, q_ref, k_ref, v_ref, o_ref, lse_ref,
                 qcur, kcur, vcur, kprev, vprev, ocur, lcur, st_a, st_b):
    c = pl.program_id(2)
    chunk = q_ref.shape[2]
    n_pairs = q_ref.shape[1]
    per_res = chunk // (dilation * BLOCK)
    n_blocks = chunk // BLOCK
    strided = dilation > 1

    @pl.when(c == 0)
    def _():
        kprev[...] = jnp.zeros_like(kprev)
        vprev[...] = jnp.zeros_like(vprev)

    qi = lax.broadcasted_iota(jnp.int32, (BLOCK, 2 * BLOCK), 0)
    kj = lax.broadcasted_iota(jnp.int32, (BLOCK, 2 * BLOCK), 1)
    dist = qi + BLOCK - kj
    band = (dist >= 0) & (dist <= BLOCK)
    band_first = band & (kj >= BLOCK * (1 - jnp.minimum(c, 1)))
    lane = lax.broadcasted_iota(jnp.int32, (BLOCK, LANES), 1)
    low_head = lane < HEAD_DIM
    ones_block = jnp.ones((2 * BLOCK, LANES), _BF16)
    zero_bf16 = jnp.zeros((BLOCK, LANES), _BF16)

    def gather(p):
        for j, (src, dst) in enumerate(((q_ref, qcur), (k_ref, kcur), (v_ref, vcur))):
            _gather_streams(dilation, src.at[0, p], dst.at[p], st_a.at[j], st_b.at[j])

    def compute(p):
        if strided:
            q_src, k_src, v_src, o_dst, l_dst = qcur.at[p], kcur.at[p], vcur.at[p], ocur.at[p], lcur
        else:
            q_src, k_src, v_src = q_ref.at[0, p], k_ref.at[0, p], v_ref.at[0, p]
            o_dst, l_dst = o_ref.at[0, p], lse_ref.at[0, 0]
        head_lanes = (lane % HEAD_DIM) == p
        for blk in range(n_blocks):
            r, i = divmod(blk, per_res)
            valid = band_first if i == 0 else band
            q = q_src[_blk_rows(blk), :]
            q2 = jnp.concatenate([jnp.where(low_head, q, zero_bf16),
                                  jnp.where(low_head, zero_bf16, q)], axis=0)
            if i == 0:
                k_prev, v_prev = kprev[p, _blk_rows(r), :], vprev[p, _blk_rows(r), :]
            else:
                k_prev, v_prev = k_src[_blk_rows(blk - 1), :], v_src[_blk_rows(blk - 1), :]
            k_band = jnp.concatenate([k_prev, k_src[_blk_rows(blk), :]], axis=0)
            v_band = jnp.concatenate([v_prev, v_src[_blk_rows(blk), :]], axis=0)
            sc = lax.dot_general(q2, k_band, (((1,), (1,)), ((), ())),
                                 preferred_element_type=_F32)
            maxes, probs = [], []
            for hh in range(HEADS_PER_PAIR):
                s_h = jnp.where(valid, sc[hh * BLOCK:(hh + 1) * BLOCK, :], -jnp.inf)
                m = jnp.max(s_h, axis=-1, keepdims=True)
                maxes.append(m)
                probs.append(jnp.exp2(s_h - m).astype(_BF16))
            v_aug = jnp.concatenate([v_band, ones_block], axis=1)
            pv = jnp.dot(jnp.concatenate(probs, axis=0), v_aug,
                         preferred_element_type=_F32)
            num = jnp.where(low_head, pv[0:BLOCK, 0:LANES], pv[BLOCK:, 0:LANES])
            den = jnp.where(low_head, pv[0:BLOCK, LANES:], pv[BLOCK:, LANES:])
            o_dst[_blk_rows(blk), :] = (num * (1.0 / den)).astype(o_dst.dtype)
            lse = jnp.where(low_head, maxes[0], maxes[1]) * LN2 + jnp.log(den)
            if p == 0:
                l_dst[_blk_rows(blk), :] = jnp.where(head_lanes, lse, 0.0)
            else:
                l_dst[_blk_rows(blk), :] = jnp.where(head_lanes, lse, l_dst[_blk_rows(blk), :])
        for r in range(dilation):
            last = r * per_res + per_res - 1
            kprev[p, _blk_rows(r), :] = k_src[_blk_rows(last), :]
            vprev[p, _blk_rows(r), :] = v_src[_blk_rows(last), :]

    def scatter(p):
        _scatter_streams(dilation, ocur.at[p], o_ref.at[0, p], st_a.at[0], st_b.at[0])

    if strided:
        gather(0)
    for p in range(n_pairs):
        if strided and p + 1 < n_pairs:
            gather(p + 1)
        compute(p)
        if strided and p >= 1:
            scatter(p - 1)
    if strided:
        scatter(n_pairs - 1)
        _scatter_streams(dilation, lcur, lse_ref.at[0, 0], st_a.at[1], st_b.at[1])


def _attention_config(q, k, v, dilation):
    b, _, s, _ = q.shape
    chunk = ATTN_CHUNK
    pg = PAIRS_PER_GROUP
    slab = lambda bi, g, c: (bi, g, c, 0)
    staged = chunk if dilation > 1 else 8
    staged_b = chunk if dilation > GATHER_STRIDE else 8
    return pl.pallas_call(
        functools.partial(_attn_kernel, dilation),
        grid=(b, PAIR_GROUPS, s // chunk),
        in_specs=[pl.BlockSpec((1, pg, chunk, LANES), slab)] * 3,
        out_specs=(
            pl.BlockSpec((1, pg, chunk, LANES), slab),
            pl.BlockSpec((1, 1, chunk, LANES), slab),
        ),
        out_shape=(
            jax.ShapeDtypeStruct((b, N_PAIRS, s, LANES), _BF16),
            jax.ShapeDtypeStruct((b, PAIR_GROUPS, s, LANES), _F32),
        ),
        scratch_shapes=[
            pltpu.VMEM((pg, staged, LANES), _BF16),
            pltpu.VMEM((pg, staged, LANES), _BF16),
            pltpu.VMEM((pg, staged, LANES), _BF16),
            pltpu.VMEM((pg, dilation * BLOCK, LANES), _BF16),
            pltpu.VMEM((pg, dilation * BLOCK, LANES), _BF16),
            pltpu.VMEM((pg, staged, LANES), _F32),
            pltpu.VMEM((staged, LANES), _F32),
            pltpu.VMEM((3, staged, LANES), _F32),
            pltpu.VMEM((3, staged_b, LANES), _F32),
        ],
        compiler_params=pltpu.CompilerParams(
            dimension_semantics=("arbitrary", "arbitrary", "arbitrary"),
            vmem_limit_bytes=VMEM_LIMIT),
        name=f"attn_d{dilation}",
    )(q, k, v)


def _mix_out_kernel(x_ref, pool_ref, o1_ref, o2_ref, o3_ref, l1_ref, l2_ref, l3_ref,
                    expand_ref, w_ref, g_ref, out_ref):
    tm = x_ref.shape[1]
    lane = lax.broadcasted_iota(jnp.int32, (tm, LANES), 1)

    def head_lse(l_ref):
        return jnp.where((lane % HEAD_DIM) < PAIRS_PER_GROUP, l_ref[0, 0],
                         pltpu.roll(l_ref[0, 1], PAIRS_PER_GROUP, axis=1))

    l1, l2, l3 = head_lse(l1_ref), head_lse(l2_ref), head_lse(l3_ref)
    mx = jnp.maximum(jnp.maximum(l1, l2), l3)
    e1, e2, e3 = jnp.exp(l1 - mx), jnp.exp(l2 - mx), jnp.exp(l3 - mx)
    inv = 1.0 / (e1 + e2 + e3)
    attn = None
    for e, o_ref in ((e1, o1_ref), (e2, o2_ref), (e3, o3_ref)):
        wexp = jnp.dot((e * inv).astype(_BF16), expand_ref[...], preferred_element_type=_F32)
        o = jnp.concatenate([o_ref[0, p] for p in range(N_PAIRS)], axis=1).astype(_F32)
        term = wexp * o
        attn = term if attn is None else attn + term
    mix = jnp.dot(pool_ref[0], w_ref[0:POOL_WIDTH, :], preferred_element_type=_F32)
    mix = mix + jnp.dot(attn.astype(_BF16), w_ref[POOL_WIDTH:, :], preferred_element_type=_F32)
    out_ref[0] = x_ref[0] + _rms_scale(mix) * g_ref[...]


def _mix_out(x, pool, outs, lses, expand, w_out, g):
    b, s, _ = x.shape
    tm = TOKEN_TILE
    row = lambda bi, i: (bi, i, 0)
    slab = lambda bi, i: (bi, 0, i, 0)
    return pl.pallas_call(
        _mix_out_kernel,
        grid=(b, s // tm),
        in_specs=[
            pl.BlockSpec((1, tm, D_MODEL), row),
            pl.BlockSpec((1, tm, POOL_WIDTH), row),
            pl.BlockSpec((1, N_PAIRS, tm, LANES), slab),
            pl.BlockSpec((1, N_PAIRS, tm, LANES), slab),
            pl.BlockSpec((1, N_PAIRS, tm, LANES), slab),
            pl.BlockSpec((1, PAIR_GROUPS, tm, LANES), slab),
            pl.BlockSpec((1, PAIR_GROUPS, tm, LANES), slab),
            pl.BlockSpec((1, PAIR_GROUPS, tm, LANES), slab),
            _resident((LANES, ATTN_WIDTH)),
            _resident((D_MODEL, D_MODEL)),
            _resident((1, D_MODEL)),
        ],
        out_specs=pl.BlockSpec((1, tm, D_MODEL), row),
        out_shape=jax.ShapeDtypeStruct((b, s, D_MODEL), _F32),
        compiler_params=pltpu.CompilerParams(
            dimension_semantics=("arbitrary", "arbitrary"), vmem_limit_bytes=VMEM_LIMIT),
        name="mix_out",
    )(x, pool, *outs, *lses, expand, w_out, g)


def _ffn_kernel(x_ref, gpre_ref, wg_ref, wu_ref, wd_ref, gpost_ref, out_ref, act_ref):
    x = x_ref[...]
    h = (_rms_scale(x) * gpre_ref[...]).astype(_BF16)
    for c in range(D_FF // FF_CHUNK):
        cols = slice(c * FF_CHUNK, (c + 1) * FF_CHUNK)
        gate = jnp.dot(h, wg_ref[:, cols], preferred_element_type=_F32)
        up = jnp.dot(h, wu_ref[:, cols], preferred_element_type=_F32)
        act_ref[:, cols] = (gate * jax.nn.sigmoid(gate) * up).astype(act_ref.dtype)
    f = jnp.dot(act_ref[...], wd_ref[...], preferred_element_type=_F32)
    out_ref[...] = x + _rms_scale(f) * gpost_ref[...]


def _ffn(x2, gpre, wg, wu, wd, gpost):
    n = x2.shape[0]
    tm = TOKEN_TILE
    row = lambda i: (i, 0)
    return pl.pallas_call(
        _ffn_kernel,
        grid=(n // tm,),
        in_specs=[
            pl.BlockSpec((tm, D_MODEL), row),
            _resident((1, D_MODEL)),
            _resident((D_MODEL, D_FF)),
            _resident((D_MODEL, D_FF)),
            _resident((D_FF, D_MODEL)),
            _resident((1, D_MODEL)),
        ],
        out_specs=pl.BlockSpec((tm, D_MODEL), row),
        out_shape=jax.ShapeDtypeStruct((n, D_MODEL), _F32),
        scratch_shapes=[pltpu.VMEM((tm, D_FF), _BF16)],
        compiler_params=pltpu.CompilerParams(
            dimension_semantics=("arbitrary",), vmem_limit_bytes=VMEM_LIMIT),
        name="ffn",
    )(x2, gpre, wg, wu, wd, gpost)


def _rope_tables(seq_len):
    freqs = ROPE_THETA ** (-jnp.arange(HALF_DIM, dtype=_F32) * (2.0 / HEAD_DIM))
    ang = jnp.arange(seq_len).astype(_F32)[:, None] * freqs[None, :]
    cos, sin = jnp.cos(ang), jnp.sin(ang)
    cos_t = jnp.concatenate([cos, cos, cos, cos], axis=1)
    sin_t = jnp.concatenate([-sin, sin, -sin, sin], axis=1)
    return cos_t, sin_t


def _block_diag(w_pool):
    n_g = w_pool.shape[0]
    out = jnp.zeros((POOL_WIDTH, POOL_WIDTH), w_pool.dtype)
    for gi in range(n_g):
        out = out.at[gi * POOL_GROUP:(gi + 1) * POOL_GROUP,
                     gi * POOL_GROUP:(gi + 1) * POOL_GROUP].set(w_pool[gi])
    return out


def _head_expand():
    e = np.zeros((LANES, ATTN_WIDTH), np.float32)
    for p in range(N_PAIRS):
        for hh in range(HEADS_PER_PAIR):
            h = HEADS_PER_PAIR * p + hh
            e[hh * HEAD_DIM + p, h * HEAD_DIM:(h + 1) * HEAD_DIM] = 1.0
    return jnp.asarray(e, dtype=_BF16)


def kernel(x, ln_pre_mix, w_in, w_pool, pool_scale, w_out, ln_post_mix, ln_pre_ffn,
           w_gate, w_up, w_down, ln_post_ffn):
    b, s, _ = x.shape
    depth = w_in.shape[0]
    cos_t, sin_t = _rope_tables(s)
    expand = _head_expand()
    for l in range(depth):
        pool, q, k, v = _in_proj(
            x, ln_pre_mix[l][None, :], w_in[l].astype(_BF16), cos_t, sin_t,
            _block_diag(w_pool[l]).astype(_BF16), pool_scale[l][None, :])
        outs, lses = [], []
        for _, dilation in DILATED_CONFIGS:
            o, lse = _attention_config(q, k, v, dilation)
            outs.append(o)
            lses.append(lse)
        x = _mix_out(x, pool, outs, lses, expand, w_out[l].astype(_BF16), ln_post_mix[l][None, :])
        x = _ffn(x.reshape(b * s, D_MODEL), ln_pre_ffn[l][None, :], w_gate[l].astype(_BF16),
                 w_up[l].astype(_BF16), w_down[l].astype(_BF16),
                 ln_post_ffn[l][None, :]).reshape(b, s, D_MODEL)
    return x
```

```python
import functools
import math

import jax
import jax.numpy as jnp
import numpy as np
from jax import lax
from jax.experimental import pallas as pl
from jax.experimental.pallas import tpu as pltpu

D_MODEL = 1024
POOL_WIDTH = 256
POOL_WINDOWS = (2, 4, 8, 16)
POOL_GROUP = 64
HEAD_DIM = 64
HALF_DIM = HEAD_DIM // 2
ATTN_WIDTH = 768
N_HEADS = 12
DILATED_CONFIGS = ((128, 1), (512, 4), (2048, 16))
BLOCK = 128
ROPE_THETA = 10000.0
IN_WIDTH = POOL_WIDTH + 3 * ATTN_WIDTH
D_FF = 2816
EPS = 1e-6

LANES = 128
MAX_POOL_WIN = max(POOL_WINDOWS)

N_PAIRS = ATTN_WIDTH // LANES
HEADS_PER_PAIR = LANES // HEAD_DIM

TOKEN_TILE = 512
POST_TILE = 512
POST_SUBTILES = 1
ATTN_CHUNK = 2048
PAIR_GROUPS = 2
PAIRS_PER_GROUP = N_PAIRS // PAIR_GROUPS
GATHER_STRIDE = 4
FF_CHUNK = 256
VMEM_LIMIT = 56 * 1024 * 1024

LOG2E = math.log2(math.e)
LN2 = math.log(2.0)

_BF16 = jnp.bfloat16
_F32 = jnp.float32


def _rms_scale(x):
    return x * lax.rsqrt(jnp.mean(x * x, axis=-1, keepdims=True) + EPS)


def _resident(shape):
    nd = len(shape)
    return pl.BlockSpec(shape, lambda *_: (0,) * nd, pipeline_mode=pl.Buffered(1))


def _rope_slab(xc, cos, sin_signed, first_half):
    ahead = pltpu.roll(xc, LANES - HALF_DIM, axis=1)
    behind = pltpu.roll(xc, HALF_DIM, axis=1)
    partner = jnp.where(first_half, ahead, behind)
    return xc * cos + partner * sin_signed


def _in_proj_kernel(x_ref, g_ref, w_ref, cos_ref, sin_ref, wpool_ref, pscale_ref,
                    pool_ref, q_ref, k_ref, v_ref, halo_ref):
    i = pl.program_id(1)
    tm = x_ref.shape[1]
    x = x_ref[0]
    h = (_rms_scale(x) * g_ref[...]).astype(_BF16)

    u = jnp.dot(h, w_ref[:, 0:POOL_WIDTH], preferred_element_type=_F32)

    @pl.when(i == 0)
    def _():
        halo_ref[...] = jnp.zeros_like(halo_ref)

    ext = jnp.concatenate([halo_ref[...], u], axis=0)
    halo_ref[...] = u[tm - MAX_POOL_WIN:, :]
    s2 = ext[1:, :] + ext[:-1, :]
    s4 = s2[2:, :] + s2[:-2, :]
    s8 = s4[4:, :] + s4[:-4, :]
    s16 = s8[8:, :] + s8[:-8, :]
    w2 = s2[MAX_POOL_WIN - 1:MAX_POOL_WIN - 1 + tm, :]
    w4 = s4[MAX_POOL_WIN - 3:MAX_POOL_WIN - 3 + tm, :]
    w8 = s8[MAX_POOL_WIN - 7:MAX_POOL_WIN - 7 + tm, :]
    w16 = s16[1:1 + tm, :]
    lane = lax.broadcasted_iota(jnp.int32, (tm, POOL_WIDTH), 1)
    grp = lane // POOL_GROUP
    wsum = jnp.where(grp == 0, w2, jnp.where(grp == 1, w4, jnp.where(grp == 2, w8, w16)))
    win = jnp.where(grp == 0, 2, jnp.where(grp == 1, 4, jnp.where(grp == 2, 8, 16)))
    pos = i * tm + lax.broadcasted_iota(jnp.int32, (tm, POOL_WIDTH), 0)
    cnt = jnp.minimum(pos + 1, win).astype(_F32)
    d = wsum / cnt - u
    y = jnp.dot(d.astype(_BF16), wpool_ref[...], preferred_element_type=_F32)
    pool_ref[0] = (y * pscale_ref[...]).astype(pool_ref.dtype)

    cos = cos_ref[...]
    sin_signed = sin_ref[...]
    lane128 = lax.broadcasted_iota(jnp.int32, (tm, LANES), 1)
    first_half = (lane128 % HEAD_DIM) < HALF_DIM
    q_scale = (HEAD_DIM ** -0.5) * LOG2E
    q0, k0, v0 = POOL_WIDTH, POOL_WIDTH + ATTN_WIDTH, POOL_WIDTH + 2 * ATTN_WIDTH
    q = jnp.dot(h, w_ref[:, q0:k0], preferred_element_type=_F32)
    for p in range(N_PAIRS):
        q_ref[0, p] = (_rope_slab(q[:, p * LANES:(p + 1) * LANES], cos, sin_signed, first_half)
                       * q_scale).astype(q_ref.dtype)
    k = jnp.dot(h, w_ref[:, k0:v0], preferred_element_type=_F32)
    for p in range(N_PAIRS):
        k_ref[0, p] = _rope_slab(k[:, p * LANES:(p + 1) * LANES], cos, sin_signed,
                                 first_half).astype(k_ref.dtype)
    v = jnp.dot(h, w_ref[:, v0:], preferred_element_type=_F32)
    for p in range(N_PAIRS):
        v_ref[0, p] = v[:, p * LANES:(p + 1) * LANES].astype(v_ref.dtype)


def _in_proj(x, g, w_in, cos_t, sin_t, wpool_bd, pscale):
    b, s, _ = x.shape
    tm = TOKEN_TILE
    row = lambda bi, i: (bi, i, 0)
    tab = lambda bi, i: (i, 0)
    slab = lambda bi, i: (bi, 0, i, 0)
    slab_shape = jax.ShapeDtypeStruct((b, N_PAIRS, s, LANES), _BF16)
    return pl.pallas_call(
        _in_proj_kernel,
        grid=(b, s // tm),
        in_specs=[
            pl.BlockSpec((1, tm, D_MODEL), row),
            _resident((1, D_MODEL)),
            _resident((D_MODEL, IN_WIDTH)),
            pl.BlockSpec((tm, LANES), tab),
            pl.BlockSpec((tm, LANES), tab),
            _resident((POOL_WIDTH, POOL_WIDTH)),
            _resident((1, POOL_WIDTH)),
        ],
        out_specs=(
            pl.BlockSpec((1, tm, POOL_WIDTH), row),
            pl.BlockSpec((1, N_PAIRS, tm, LANES), slab),
            pl.BlockSpec((1, N_PAIRS, tm, LANES), slab),
            pl.BlockSpec((1, N_PAIRS, tm, LANES), slab),
        ),
        out_shape=(jax.ShapeDtypeStruct((b, s, POOL_WIDTH), _BF16), slab_shape, slab_shape, slab_shape),
        scratch_shapes=[pltpu.VMEM((MAX_POOL_WIN, POOL_WIDTH), _F32)],
        compiler_params=pltpu.CompilerParams(
            dimension_semantics=("arbitrary", "arbitrary"), vmem_limit_bytes=VMEM_LIMIT),
        name="in_proj",
    )(x, g, w_in, cos_t, sin_t, wpool_bd, pscale)


def _blk_rows(blk):
    return slice(blk * BLOCK, (blk + 1) * BLOCK)


def _gather_streams(dilation, src, dst, st_a, st_b):
    chunk = src.shape[0]
    st_a[...] = src[...].astype(_F32)
    if dilation == GATHER_STRIDE:
        per_res = chunk // dilation
        for r in range(dilation):
            dst[r * per_res:(r + 1) * per_res, :] = (
                st_a[pl.ds(r, per_res, stride=GATHER_STRIDE), :].astype(dst.dtype))
    else:
        quarter = chunk // GATHER_STRIDE
        per_res = chunk // dilation
        for a in range(GATHER_STRIDE):
            st_b[a * quarter:(a + 1) * quarter, :] = st_a[pl.ds(a, quarter, stride=GATHER_STRIDE), :]
        for r in range(dilation):
            a, b = r % GATHER_STRIDE, r // GATHER_STRIDE
            dst[r * per_res:(r + 1) * per_res, :] = (
                st_b[pl.ds(a * quarter + b, per_res, stride=GATHER_STRIDE), :].astype(dst.dtype))


def _scatter_streams(dilation, src, dst, st_a, st_b):
    chunk = src.shape[0]
    if dilation == GATHER_STRIDE:
        per_res = chunk // dilation
        for r in range(dilation):
            st_a[pl.ds(r, per_res, stride=GATHER_STRIDE), :] = src[r * per_res:(r + 1) * per_res, :]
    else:
        quarter = chunk // GATHER_STRIDE
        per_res = chunk // dilation
        for r in range(dilation):
            a, b = r % GATHER_STRIDE, r // GATHER_STRIDE
            st_b[pl.ds(a * quarter + b, per_res, stride=GATHER_STRIDE), :] = (
                src[r * per_res:(r + 1) * per_res, :])
        for a in range(GATHER_STRIDE):
            st_a[pl.ds(a, quarter, stride=GATHER_STRIDE), :] = st_b[a * quarter:(a + 1) * quarter, :]
    dst[...] = st_a[...].astype(dst.dtype)


def _attn_kernel(dilation, q_ref, k_ref, v_ref, o_ref, lse_ref,
                 qcur, kcur, vcur, kprev, vprev, ocur, lcur, st_a, st_b):
    c = pl.program_id(2)
    chunk = q_ref.shape[2]
    n_pairs = q_ref.shape[1]
    per_res = chunk // (dilation * BLOCK)
    n_blocks = chunk // BLOCK
    strided = dilation > 1

    @pl.when(c == 0)
    def _():
        kprev[...] = jnp.zeros_like(kprev)
        vprev[...] = jnp.zeros_like(vprev)

    qi = lax.broadcasted_iota(jnp.int32, (BLOCK, 2 * BLOCK), 0)
    kj = lax.broadcasted_iota(jnp.int32, (BLOCK, 2 * BLOCK), 1)
    dist = qi + BLOCK - kj
    band = (dist >= 0) & (dist <= BLOCK)
    band_first = band & (kj >= BLOCK * (1 - jnp.minimum(c, 1)))
    lane = lax.broadcasted_iota(jnp.int32, (BLOCK, LANES), 1)
    low_head = lane < HEAD_DIM
    ones_block = jnp.ones((2 * BLOCK, LANES), _BF16)
    zero_bf16 = jnp.zeros((BLOCK, LANES), _BF16)

    def gather(p):
        for j, (src, dst) in enumerate(((q_ref, qcur), (k_ref, kcur), (v_ref, vcur))):
            _gather_streams(dilation, src.at[0, p], dst.at[p], st_a.at[j], st_b.at[j])

    def compute(p):
        if strided:
            q_src, k_src, v_src, o_dst, l_dst = qcur.at[p], kcur.at[p], vcur.at[p], ocur.at[p], lcur
        else:
            q_src, k_src, v_src = q_ref.at[0, p], k_ref.at[0, p], v_ref.at[0, p]
            o_dst, l_dst = o_ref.at[0, p], lse_ref.at[0, 0]
        head_lanes = (lane % HEAD_DIM) == p
        for blk in range(n_blocks):
            r, i = divmod(blk, per_res)
            valid = band_first if i == 0 else band
            q = q_src[_blk_rows(blk), :]
            q2 = jnp.concatenate([jnp.where(low_head, q, zero_bf16),
                                  jnp.where(low_head, zero_bf16, q)], axis=0)
            if i == 0:
                k_prev, v_prev = kprev[p, _blk_rows(r), :], vprev[p, _blk_rows(r), :]
            else:
                k_prev, v_prev = k_src[_blk_rows(blk - 1), :], v_src[_blk_rows(blk - 1), :]
            k_band = jnp.concatenate([k_prev, k_src[_blk_rows(blk), :]], axis=0)
            v_band = jnp.concatenate([v_prev, v_src[_blk_rows(blk), :]], axis=0)
            sc = lax.dot_general(q2, k_band, (((1,), (1,)), ((), ())),
                                 preferred_element_type=_F32)
            maxes, probs = [], []
            for hh in range(HEADS_PER_PAIR):
                s_h = jnp.where(valid, sc[hh * BLOCK:(hh + 1) * BLOCK, :], -jnp.inf)
                m = jnp.max(s_h, axis=-1, keepdims=True)
                maxes.append(m)
                probs.append(jnp.exp2(s_h - m).astype(_BF16))
            v_aug = jnp.concatenate([v_band, ones_block], axis=1)
            pv = jnp.dot(jnp.concatenate(probs, axis=0), v_aug,
                         preferred_element_type=_F32)
            num = jnp.where(low_head, pv[0:BLOCK, 0:LANES], pv[BLOCK:, 0:LANES])
            den = jnp.where(low_head, pv[0:BLOCK, LANES:], pv[BLOCK:, LANES:])
            o_dst[_blk_rows(blk), :] = (num * (1.0 / den)).astype(o_dst.dtype)
            lse = jnp.where(low_head, maxes[0], maxes[1]) * LN2 + jnp.log(den)
            if p == 0:
                l_dst[_blk_rows(blk), :] = jnp.where(head_lanes, lse, 0.0)
            else:
                l_dst[_blk_rows(blk), :] = jnp.where(head_lanes, lse, l_dst[_blk_rows(blk), :])
        for r in range(dilation):
            last = r * per_res + per_res - 1
            kprev[p, _blk_rows(r), :] = k_src[_blk_rows(last), :]
            vprev[p, _blk_rows(r), :] = v_src[_blk_rows(last), :]

    def scatter(p):
        _scatter_streams(dilation, ocur.at[p], o_ref.at[0, p], st_a.at[0], st_b.at[0])

    if strided:
        gather(0)
    for p in range(n_pairs):
        if strided and p + 1 < n_pairs:
            gather(p + 1)
        compute(p)
        if strided and p >= 1:
            scatter(p - 1)
    if strided:
        scatter(n_pairs - 1)
        _scatter_streams(dilation, lcur, lse_ref.at[0, 0], st_a.at[1], st_b.at[1])


def _attention_config(q, k, v, dilation):
    b, _, s, _ = q.shape
    chunk = ATTN_CHUNK
    pg = PAIRS_PER_GROUP
    slab = lambda bi, g, c: (bi, g, c, 0)
    staged = chunk if dilation > 1 else 8
    staged_b = chunk if dilation > GATHER_STRIDE else 8
    return pl.pallas_call(
        functools.partial(_attn_kernel, dilation),
        grid=(b, PAIR_GROUPS, s // chunk),
        in_specs=[pl.BlockSpec((1, pg, chunk, LANES), slab)] * 3,
        out_specs=(
            pl.BlockSpec((1, pg, chunk, LANES), slab),
            pl.BlockSpec((1, 1, chunk, LANES), slab),
        ),
        out_shape=(
            jax.ShapeDtypeStruct((b, N_PAIRS, s, LANES), _BF16),
            jax.ShapeDtypeStruct((b, PAIR_GROUPS, s, LANES), _F32),
        ),
        scratch_shapes=[
            pltpu.VMEM((pg, staged, LANES), _BF16),
            pltpu.VMEM((pg, staged, LANES), _BF16),
            pltpu.VMEM((pg, staged, LANES), _BF16),
            pltpu.VMEM((pg, dilation * BLOCK, LANES), _BF16),
            pltpu.VMEM((pg, dilation * BLOCK, LANES), _BF16),
            pltpu.VMEM((pg, staged, LANES), _F32),
            pltpu.VMEM((staged, LANES), _F32),
            pltpu.VMEM((3, staged, LANES), _F32),
            pltpu.VMEM((3, staged_b, LANES), _F32),
        ],
        compiler_params=pltpu.CompilerParams(
            dimension_semantics=("arbitrary", "arbitrary", "arbitrary"),
            vmem_limit_bytes=VMEM_LIMIT),
        name=f"attn_d{dilation}",
    )(q, k, v)


def _merge_and_project(rows, x_ref, pool_ref, o_refs, l_refs, expand_ref, w_ref, g_ref):
    n_rows = rows.stop - rows.start
    lane = lax.broadcasted_iota(jnp.int32, (n_rows, LANES), 1)

    def head_lse(l_ref):
        return jnp.where((lane % HEAD_DIM) < PAIRS_PER_GROUP, l_ref[0, 0, rows, :],
                         pltpu.roll(l_ref[0, 1, rows, :], PAIRS_PER_GROUP, axis=1))

    l1, l2, l3 = (head_lse(l_ref) for l_ref in l_refs)
    mx = jnp.maximum(jnp.maximum(l1, l2), l3)
    e1, e2, e3 = jnp.exp(l1 - mx), jnp.exp(l2 - mx), jnp.exp(l3 - mx)
    inv = 1.0 / (e1 + e2 + e3)
    attn = None
    for e, o_ref in zip((e1, e2, e3), o_refs):
        wexp = jnp.dot((e * inv).astype(_BF16), expand_ref[...], preferred_element_type=_F32)
        o = jnp.concatenate([o_ref[0, p, rows, :] for p in range(N_PAIRS)], axis=1).astype(_F32)
        term = wexp * o
        attn = term if attn is None else attn + term
    mix = jnp.dot(pool_ref[0, rows, :], w_ref[0:POOL_WIDTH, :], preferred_element_type=_F32)
    mix = mix + jnp.dot(attn.astype(_BF16), w_ref[POOL_WIDTH:, :], preferred_element_type=_F32)
    return x_ref[0, rows, :] + _rms_scale(mix) * g_ref[...]


def _swiglu(x, gpre_ref, wg_ref, wu_ref, wd_ref, gpost_ref, act_ref):
    h = (_rms_scale(x) * gpre_ref[...]).astype(_BF16)
    for c in range(D_FF // FF_CHUNK):
        cols = slice(c * FF_CHUNK, (c + 1) * FF_CHUNK)
        gate = jnp.dot(h, wg_ref[:, cols], preferred_element_type=_F32)
        up = jnp.dot(h, wu_ref[:, cols], preferred_element_type=_F32)
        act_ref[:, cols] = (gate * jax.nn.sigmoid(gate) * up).astype(act_ref.dtype)
    f = jnp.dot(act_ref[...], wd_ref[...], preferred_element_type=_F32)
    return x + _rms_scale(f) * gpost_ref[...]


def _post_kernel(x_ref, pool_ref, o1_ref, o2_ref, o3_ref, l1_ref, l2_ref, l3_ref, expand_ref,
                 wout_ref, gmix_ref, gpre_ref, wg_ref, wu_ref, wd_ref, gpost_ref, out_ref, act_ref):
    tm = x_ref.shape[1]
    sub = tm // POST_SUBTILES
    for s in range(POST_SUBTILES):
        rows = slice(s * sub, (s + 1) * sub)
        x1 = _merge_and_project(rows, x_ref, pool_ref, (o1_ref, o2_ref, o3_ref),
                                (l1_ref, l2_ref, l3_ref), expand_ref, wout_ref, gmix_ref)
        out_ref[0, rows, :] = _swiglu(x1, gpre_ref, wg_ref, wu_ref, wd_ref, gpost_ref, act_ref.at[s])


def _post(x, pool, outs, lses, expand, w_out, gmix, gpre, wg, wu, wd, gpost):
    b, s, _ = x.shape
    tm = POST_TILE
    row = lambda bi, i: (bi, i, 0)
    slab = lambda bi, i: (bi, 0, i, 0)
    return pl.pallas_call(
        _post_kernel,
        grid=(b, s // tm),
        in_specs=[
            pl.BlockSpec((1, tm, D_MODEL), row),
            pl.BlockSpec((1, tm, POOL_WIDTH), row),
            pl.BlockSpec((1, N_PAIRS, tm, LANES), slab),
            pl.BlockSpec((1, N_PAIRS, tm, LANES), slab),
            pl.BlockSpec((1, N_PAIRS, tm, LANES), slab),
            pl.BlockSpec((1, PAIR_GROUPS, tm, LANES), slab),
            pl.BlockSpec((1, PAIR_GROUPS, tm, LANES), slab),
            pl.BlockSpec((1, PAIR_GROUPS, tm, LANES), slab),
            _resident((LANES, ATTN_WIDTH)),
            _resident((D_MODEL, D_MODEL)),
            _resident((1, D_MODEL)),
            _resident((1, D_MODEL)),
            _resident((D_MODEL, D_FF)),
            _resident((D_MODEL, D_FF)),
            _resident((D_FF, D_MODEL)),
            _resident((1, D_MODEL)),
        ],
        out_specs=pl.BlockSpec((1, tm, D_MODEL), row),
        out_shape=jax.ShapeDtypeStruct((b, s, D_MODEL), _F32),
        scratch_shapes=[pltpu.VMEM((POST_SUBTILES, tm // POST_SUBTILES, D_FF), _BF16)],
        compiler_params=pltpu.CompilerParams(
            dimension_semantics=("arbitrary", "arbitrary"), vmem_limit_bytes=VMEM_LIMIT),
        name="post",
    )(x, pool, *outs, *lses, expand, w_out, gmix, gpre, wg, wu, wd, gpost)


def _rope_tables(seq_len):
    freqs = ROPE_THETA ** (-jnp.arange(HALF_DIM, dtype=_F32) * (2.0 / HEAD_DIM))
    ang = jnp.arange(seq_len).astype(_F32)[:, None] * freqs[None, :]
    cos, sin = jnp.cos(ang), jnp.sin(ang)
    cos_t = jnp.concatenate([cos, cos, cos, cos], axis=1)
    sin_t = jnp.concatenate([-sin, sin, -sin, sin], axis=1)
    return cos_t, sin_t


def _block_diag(w_pool):
    n_g = w_pool.shape[0]
    out = jnp.zeros((POOL_WIDTH, POOL_WIDTH), w_pool.dtype)
    for gi in range(n_g):
        out = out.at[gi * POOL_GROUP:(gi + 1) * POOL_GROUP,
                     gi * POOL_GROUP:(gi + 1) * POOL_GROUP].set(w_pool[gi])
    return out


def _head_expand():
    e = np.zeros((LANES, ATTN_WIDTH), np.float32)
    for p in range(N_PAIRS):
        for hh in range(HEADS_PER_PAIR):
            h = HEADS_PER_PAIR * p + hh
            e[hh * HEAD_DIM + p, h * HEAD_DIM:(h + 1) * HEAD_DIM] = 1.0
    return jnp.asarray(e, dtype=_BF16)


def kernel(x, ln_pre_mix, w_in, w_pool, pool_scale, w_out, ln_post_mix, ln_pre_ffn,
           w_gate, w_up, w_down, ln_post_ffn):
    b, s, _ = x.shape
    depth = w_in.shape[0]
    cos_t, sin_t = _rope_tables(s)
    expand = _head_expand()
    for l in range(depth):
        pool, q, k, v = _in_proj(
            x, ln_pre_mix[l][None, :], w_in[l].astype(_BF16), cos_t, sin_t,
            _block_diag(w_pool[l]).astype(_BF16), pool_scale[l][None, :])
        outs, lses = [], []
        for _, dilation in DILATED_CONFIGS:
            o, lse = _attention_config(q, k, v, dilation)
            outs.append(o)
            lses.append(lse)
        x = _post(x, pool, outs, lses, expand, w_out[l].astype(_BF16), ln_post_mix[l][None, :],
                  ln_pre_ffn[l][None, :], w_gate[l].astype(_BF16), w_up[l].astype(_BF16),
                  w_down[l].astype(_BF16), ln_post_ffn[l][None, :])
    return x
```

```python
import functools
import math

import jax
import jax.numpy as jnp
import numpy as np
from jax import lax
from jax.experimental import pallas as pl
from jax.experimental.pallas import tpu as pltpu

D_MODEL = 1024
POOL_WIDTH = 256
POOL_WINDOWS = (2, 4, 8, 16)
POOL_GROUP = 64
HEAD_DIM = 64
HALF_DIM = HEAD_DIM // 2
ATTN_WIDTH = 768
N_HEADS = 12
DILATED_CONFIGS = ((128, 1), (512, 4), (2048, 16))
BLOCK = 128
ROPE_THETA = 10000.0
IN_WIDTH = POOL_WIDTH + 3 * ATTN_WIDTH
D_FF = 2816
EPS = 1e-6

LANES = 128
MAX_POOL_WIN = max(POOL_WINDOWS)

N_PAIRS = ATTN_WIDTH // LANES
HEADS_PER_PAIR = LANES // HEAD_DIM

TOKEN_TILE = 1024
IN_SUBTILES = 2
POST_TILE = 512
ATTN_CHUNK = 2048
PAIR_GROUPS = 2
PAIRS_PER_GROUP = N_PAIRS // PAIR_GROUPS
GATHER_STRIDE = 4
FF_CHUNK = 256
VMEM_LIMIT = 56 * 1024 * 1024

LOG2E = math.log2(math.e)
LN2 = math.log(2.0)

_BF16 = jnp.bfloat16
_F32 = jnp.float32


def _rms_scale(x):
    return x * lax.rsqrt(jnp.mean(x * x, axis=-1, keepdims=True) + EPS)


def _resident(shape):
    nd = len(shape)
    return pl.BlockSpec(shape, lambda *_: (0,) * nd, pipeline_mode=pl.Buffered(1))


def _emit_interleaved(main, side):
    live = [g for g in (main, side) if g is not None]
    while live:
        for g in list(live):
            if next(g, StopIteration) is StopIteration:
                live.remove(g)


def _rope_slab(xc, cos, sin_signed, first_half):
    ahead = pltpu.roll(xc, LANES - HALF_DIM, axis=1)
    behind = pltpu.roll(xc, HALF_DIM, axis=1)
    partner = jnp.where(first_half, ahead, behind)
    return xc * cos + partner * sin_signed


def _pool_mixer(u, first_pos, halo_ref, wpool_ref, pscale_ref):
    n = u.shape[0]
    ext = jnp.concatenate([halo_ref[...], u], axis=0)
    halo_ref[...] = u[n - MAX_POOL_WIN:, :]
    s2 = ext[1:, :] + ext[:-1, :]
    s4 = s2[2:, :] + s2[:-2, :]
    s8 = s4[4:, :] + s4[:-4, :]
    s16 = s8[8:, :] + s8[:-8, :]
    w2 = s2[MAX_POOL_WIN - 1:MAX_POOL_WIN - 1 + n, :]
    w4 = s4[MAX_POOL_WIN - 3:MAX_POOL_WIN - 3 + n, :]
    w8 = s8[MAX_POOL_WIN - 7:MAX_POOL_WIN - 7 + n, :]
    w16 = s16[1:1 + n, :]
    lane = lax.broadcasted_iota(jnp.int32, (n, POOL_WIDTH), 1)
    grp = lane // POOL_GROUP
    wsum = jnp.where(grp == 0, w2, jnp.where(grp == 1, w4, jnp.where(grp == 2, w8, w16)))
    win = jnp.where(grp == 0, 2, jnp.where(grp == 1, 4, jnp.where(grp == 2, 8, 16)))
    pos = first_pos + lax.broadcasted_iota(jnp.int32, (n, POOL_WIDTH), 0)
    cnt = jnp.minimum(pos + 1, win).astype(_F32)
    d = wsum / cnt - u
    y = jnp.dot(d.astype(_BF16), wpool_ref[...], preferred_element_type=_F32)
    return y * pscale_ref[...]


def _project_stages(rows, x_ref, g_ref, w_ref, proj):
    h = (_rms_scale(x_ref[0, rows, :]) * g_ref[...]).astype(_BF16)
    yield
    edges = (0, POOL_WIDTH, POOL_WIDTH + ATTN_WIDTH, POOL_WIDTH + 2 * ATTN_WIDTH, IN_WIDTH)
    for name, lo, hi in zip(("u", "q", "k", "v"), edges[:-1], edges[1:]):
        proj[name] = jnp.dot(h, w_ref[:, lo:hi], preferred_element_type=_F32)
        yield


def _finish_stages(rows, first_pos, proj, cos_ref, sin_ref, wpool_ref, pscale_ref, halo_ref,
                   pool_ref, q_ref, k_ref, v_ref):
    n = rows.stop - rows.start
    pool_ref[0, rows, :] = _pool_mixer(proj["u"], first_pos, halo_ref, wpool_ref,
                                       pscale_ref).astype(pool_ref.dtype)
    yield
    cos = cos_ref[rows, :]
    sin_signed = sin_ref[rows, :]
    lane128 = lax.broadcasted_iota(jnp.int32, (n, LANES), 1)
    first_half = (lane128 % HEAD_DIM) < HALF_DIM
    q_scale = (HEAD_DIM ** -0.5) * LOG2E
    for name, ref, scale in (("q", q_ref, q_scale), ("k", k_ref, None)):
        for p in range(N_PAIRS):
            t = _rope_slab(proj[name][:, p * LANES:(p + 1) * LANES], cos, sin_signed, first_half)
            ref[0, p, rows, :] = (t if scale is None else t * scale).astype(ref.dtype)
            if p % 2 == 1:
                yield
    for p in range(N_PAIRS):
        v_ref[0, p, rows, :] = proj["v"][:, p * LANES:(p + 1) * LANES].astype(v_ref.dtype)


def _in_proj_kernel(x_ref, g_ref, w_ref, cos_ref, sin_ref, wpool_ref, pscale_ref,
                    pool_ref, q_ref, k_ref, v_ref, halo_ref):
    i = pl.program_id(1)
    tm = x_ref.shape[1]
    sub = tm // IN_SUBTILES
    rows = [slice(s * sub, (s + 1) * sub) for s in range(IN_SUBTILES)]
    projs = [{} for _ in range(IN_SUBTILES)]

    @pl.when(i == 0)
    def _():
        halo_ref[...] = jnp.zeros_like(halo_ref)

    def project(s):
        return _project_stages(rows[s], x_ref, g_ref, w_ref, projs[s])

    def finish(s):
        return _finish_stages(rows[s], i * tm + s * sub, projs[s], cos_ref, sin_ref, wpool_ref,
                              pscale_ref, halo_ref, pool_ref, q_ref, k_ref, v_ref)

    _emit_interleaved(project(0), None)
    for s in range(IN_SUBTILES):
        _emit_interleaved(project(s + 1) if s + 1 < IN_SUBTILES else None, finish(s))


def _in_proj(x, g, w_in, cos_t, sin_t, wpool_bd, pscale):
    b, s, _ = x.shape
    tm = TOKEN_TILE
    row = lambda bi, i: (bi, i, 0)
    tab = lambda bi, i: (i, 0)
    slab = lambda bi, i: (bi, 0, i, 0)
    slab_shape = jax.ShapeDtypeStruct((b, N_PAIRS, s, LANES), _BF16)
    return pl.pallas_call(
        _in_proj_kernel,
        grid=(b, s // tm),
        in_specs=[
            pl.BlockSpec((1, tm, D_MODEL), row),
            _resident((1, D_MODEL)),
            _resident((D_MODEL, IN_WIDTH)),
            pl.BlockSpec((tm, LANES), tab),
            pl.BlockSpec((tm, LANES), tab),
            _resident((POOL_WIDTH, POOL_WIDTH)),
            _resident((1, POOL_WIDTH)),
        ],
        out_specs=(
            pl.BlockSpec((1, tm, POOL_WIDTH), row),
            pl.BlockSpec((1, N_PAIRS, tm, LANES), slab),
            pl.BlockSpec((1, N_PAIRS, tm, LANES), slab),
            pl.BlockSpec((1, N_PAIRS, tm, LANES), slab),
        ),
        out_shape=(jax.ShapeDtypeStruct((b, s, POOL_WIDTH), _BF16), slab_shape, slab_shape, slab_shape),
        scratch_shapes=[pltpu.VMEM((MAX_POOL_WIN, POOL_WIDTH), _F32)],
        compiler_params=pltpu.CompilerParams(
            dimension_semantics=("arbitrary", "arbitrary"), vmem_limit_bytes=VMEM_LIMIT),
        name="in_proj",
    )(x, g, w_in, cos_t, sin_t, wpool_bd, pscale)


def _blk_rows(blk):
    return slice(blk * BLOCK, (blk + 1) * BLOCK)


def _gather_streams(dilation, src, dst, st_a, st_b):
    chunk = src.shape[0]
    st_a[...] = src[...].astype(_F32)
    if dilation == GATHER_STRIDE:
        per_res = chunk // dilation
        for r in range(dilation):
            dst[r * per_res:(r + 1) * per_res, :] = (
                st_a[pl.ds(r, per_res, stride=GATHER_STRIDE), :].astype(dst.dtype))
    else:
        quarter = chunk // GATHER_STRIDE
        per_res = chunk // dilation
        for a in range(GATHER_STRIDE):
            st_b[a * quarter:(a + 1) * quarter, :] = st_a[pl.ds(a, quarter, stride=GATHER_STRIDE), :]
        for r in range(dilation):
            a, b = r % GATHER_STRIDE, r // GATHER_STRIDE
            dst[r * per_res:(r + 1) * per_res, :] = (
                st_b[pl.ds(a * quarter + b, per_res, stride=GATHER_STRIDE), :].astype(dst.dtype))


def _scatter_streams(dilation, src, dst, st_a, st_b):
    chunk = src.shape[0]
    if dilation == GATHER_STRIDE:
        per_res = chunk // dilation
        for r in range(dilation):
            st_a[pl.ds(r, per_res, stride=GATHER_STRIDE), :] = src[r * per_res:(r + 1) * per_res, :]
    else:
        quarter = chunk // GATHER_STRIDE
        per_res = chunk // dilation
        for r in range(dilation):
            a, b = r % GATHER_STRIDE, r // GATHER_STRIDE
            st_b[pl.ds(a * quarter + b, per_res, stride=GATHER_STRIDE), :] = (
                src[r * per_res:(r + 1) * per_res, :])
        for a in range(GATHER_STRIDE):
            st_a[pl.ds(a, quarter, stride=GATHER_STRIDE), :] = st_b[a * quarter:(a + 1) * quarter, :]
    dst[...] = st_a[...].astype(dst.dtype)


def _attn_kernel(dilation, q_ref, k_ref, v_ref, o_ref, lse_ref,
                 qcur, kcur, vcur, kprev, vprev, ocur, lcur, st_a, st_b):
    c = pl.program_id(2)
    chunk = q_ref.shape[2]
    n_pairs = q_ref.shape[1]
    per_res = chunk // (dilation * BLOCK)
    n_blocks = chunk // BLOCK
    strided = dilation > 1

    @pl.when(c == 0)
    def _():
        kprev[...] = jnp.zeros_like(kprev)
        vprev[...] = jnp.zeros_like(vprev)

    qi = lax.broadcasted_iota(jnp.int32, (BLOCK, 2 * BLOCK), 0)
    kj = lax.broadcasted_iota(jnp.int32, (BLOCK, 2 * BLOCK), 1)
    dist = qi + BLOCK - kj
    band = (dist >= 0) & (dist <= BLOCK)
    band_first = band & (kj >= BLOCK * (1 - jnp.minimum(c, 1)))
    lane = lax.broadcasted_iota(jnp.int32, (BLOCK, LANES), 1)
    low_head = lane < HEAD_DIM
    ones_block = jnp.ones((2 * BLOCK, LANES), _BF16)
    zero_bf16 = jnp.zeros((BLOCK, LANES), _BF16)

    def gather(p):
        for j, (src, dst) in enumerate(((q_ref, qcur), (k_ref, kcur), (v_ref, vcur))):
            _gather_streams(dilation, src.at[0, p], dst.at[p], st_a.at[j], st_b.at[j])

    def compute(p):
        if strided:
            q_src, k_src, v_src, o_dst, l_dst = qcur.at[p], kcur.at[p], vcur.at[p], ocur.at[p], lcur
        else:
            q_src, k_src, v_src = q_ref.at[0, p], k_ref.at[0, p], v_ref.at[0, p]
            o_dst, l_dst = o_ref.at[0, p], lse_ref.at[0, 0]
        head_lanes = (lane % HEAD_DIM) == p
        for blk in range(n_blocks):
            r, i = divmod(blk, per_res)
            valid = band_first if i == 0 else band
            q = q_src[_blk_rows(blk), :]
            q2 = jnp.concatenate([jnp.where(low_head, q, zero_bf16),
                                  jnp.where(low_head, zero_bf16, q)], axis=0)
            if i == 0:
                k_prev, v_prev = kprev[p, _blk_rows(r), :], vprev[p, _blk_rows(r), :]
            else:
                k_prev, v_prev = k_src[_blk_rows(blk - 1), :], v_src[_blk_rows(blk - 1), :]
            k_band = jnp.concatenate([k_prev, k_src[_blk_rows(blk), :]], axis=0)
            v_band = jnp.concatenate([v_prev, v_src[_blk_rows(blk), :]], axis=0)
            sc = lax.dot_general(q2, k_band, (((1,), (1,)), ((), ())),
                                 preferred_element_type=_F32)
            maxes, probs = [], []
            for hh in range(HEADS_PER_PAIR):
                s_h = jnp.where(valid, sc[hh * BLOCK:(hh + 1) * BLOCK, :], -jnp.inf)
                m = jnp.max(s_h, axis=-1, keepdims=True)
                maxes.append(m)
                probs.append(jnp.exp2(s_h - m).astype(_BF16))
            v_aug = jnp.concatenate([v_band, ones_block], axis=1)
            pv = jnp.dot(jnp.concatenate(probs, axis=0), v_aug,
                         preferred_element_type=_F32)
            num = jnp.where(low_head, pv[0:BLOCK, 0:LANES], pv[BLOCK:, 0:LANES])
            den = jnp.where(low_head, pv[0:BLOCK, LANES:], pv[BLOCK:, LANES:])
            o_dst[_blk_rows(blk), :] = (num * (1.0 / den)).astype(o_dst.dtype)
            lse = jnp.where(low_head, maxes[0], maxes[1]) * LN2 + jnp.log(den)
            if p == 0:
                l_dst[_blk_rows(blk), :] = jnp.where(head_lanes, lse, 0.0)
            else:
                l_dst[_blk_rows(blk), :] = jnp.where(head_lanes, lse, l_dst[_blk_rows(blk), :])
        for r in range(dilation):
            last = r * per_res + per_res - 1
            kprev[p, _blk_rows(r), :] = k_src[_blk_rows(last), :]
            vprev[p, _blk_rows(r), :] = v_src[_blk_rows(last), :]

    def scatter(p):
        _scatter_streams(dilation, ocur.at[p], o_ref.at[0, p], st_a.at[0], st_b.at[0])

    if strided:
        gather(0)
    for p in range(n_pairs):
        if strided and p + 1 < n_pairs:
            gather(p + 1)
        compute(p)
        if strided and p >= 1:
            scatter(p - 1)
    if strided:
        scatter(n_pairs - 1)
        _scatter_streams(dilation, lcur, lse_ref.at[0, 0], st_a.at[1], st_b.at[1])


def _attention_config(q, k, v, dilation):
    b, _, s, _ = q.shape
    chunk = ATTN_CHUNK
    pg = PAIRS_PER_GROUP
    slab = lambda bi, g, c: (bi, g, c, 0)
    staged = chunk if dilation > 1 else 8
    staged_b = chunk if dilation > GATHER_STRIDE else 8
    return pl.pallas_call(
        functools.partial(_attn_kernel, dilation),
        grid=(b, PAIR_GROUPS, s // chunk),
        in_specs=[pl.BlockSpec((1, pg, chunk, LANES), slab)] * 3,
        out_specs=(
            pl.BlockSpec((1, pg, chunk, LANES), slab),
            pl.BlockSpec((1, 1, chunk, LANES), slab),
        ),
        out_shape=(
            jax.ShapeDtypeStruct((b, N_PAIRS, s, LANES), _BF16),
            jax.ShapeDtypeStruct((b, PAIR_GROUPS, s, LANES), _F32),
        ),
        scratch_shapes=[
            pltpu.VMEM((pg, staged, LANES), _BF16),
            pltpu.VMEM((pg, staged, LANES), _BF16),
            pltpu.VMEM((pg, staged, LANES), _BF16),
            pltpu.VMEM((pg, dilation * BLOCK, LANES), _BF16),
            pltpu.VMEM((pg, dilation * BLOCK, LANES), _BF16),
            pltpu.VMEM((pg, staged, LANES), _F32),
            pltpu.VMEM((staged, LANES), _F32),
            pltpu.VMEM((3, staged, LANES), _F32),
            pltpu.VMEM((3, staged_b, LANES), _F32),
        ],
        compiler_params=pltpu.CompilerParams(
            dimension_semantics=("arbitrary", "arbitrary", "arbitrary"),
            vmem_limit_bytes=VMEM_LIMIT),
        name=f"attn_d{dilation}",
    )(q, k, v)


def _merge_stages(rows, x_ref, pool_ref, o_refs, l_refs, expand_ref, w_ref, result):
    n_rows = rows.stop - rows.start
    lane = lax.broadcasted_iota(jnp.int32, (n_rows, LANES), 1)

    def head_lse(l_ref):
        return jnp.where((lane % HEAD_DIM) < PAIRS_PER_GROUP, l_ref[0, 0, rows, :],
                         pltpu.roll(l_ref[0, 1, rows, :], PAIRS_PER_GROUP, axis=1))

    l1, l2, l3 = (head_lse(l_ref) for l_ref in l_refs)
    mx = jnp.maximum(jnp.maximum(l1, l2), l3)
    e1, e2, e3 = jnp.exp(l1 - mx), jnp.exp(l2 - mx), jnp.exp(l3 - mx)
    inv = 1.0 / (e1 + e2 + e3)
    weights = [(e * inv).astype(_BF16) for e in (e1, e2, e3)]
    yield
    wexps = [jnp.dot(w, expand_ref[...], preferred_element_type=_F32) for w in weights]
    mix_pool = jnp.dot(pool_ref[0, rows, :], w_ref[0:POOL_WIDTH, :], preferred_element_type=_F32)
    yield
    attn = None
    for wexp, o_ref in zip(wexps, o_refs):
        o = jnp.concatenate([o_ref[0, p, rows, :] for p in range(N_PAIRS)], axis=1).astype(_F32)
        term = wexp * o
        attn = term if attn is None else attn + term
        yield
    result["mix"] = mix_pool + jnp.dot(attn.astype(_BF16), w_ref[POOL_WIDTH:, :],
                                       preferred_element_type=_F32)


def _gate_up_stages(h_ref, wg_ref, wu_ref, act_ref):
    for c in range(D_FF // FF_CHUNK):
        cols = slice(c * FF_CHUNK, (c + 1) * FF_CHUNK)
        gate = jnp.dot(h_ref[...], wg_ref[:, cols], preferred_element_type=_F32)
        up = jnp.dot(h_ref[...], wu_ref[:, cols], preferred_element_type=_F32)
        act_ref[:, cols] = (gate * jax.nn.sigmoid(gate) * up).astype(act_ref.dtype)
        yield


def _post_kernel(x_ref, pool_ref, o1_ref, o2_ref, o3_ref, l1_ref, l2_ref, l3_ref, expand_ref,
                 wout_ref, gmix_ref, gpre_ref, wg_ref, wu_ref, wd_ref, gpost_ref, out_ref,
                 act_ref, x1_stage, h_stage, x1_prev, h_prev):
    t = pl.program_id(0)
    tm = x_ref.shape[1]

    @pl.when(t == 0)
    def _():
        x1_stage[...] = jnp.zeros_like(x1_stage)
        h_stage[...] = jnp.zeros_like(h_stage)

    x1_prev[...] = x1_stage[...]
    h_prev[...] = h_stage[...]

    def stage_tile():
        result = {}
        yield from _merge_stages(slice(0, tm), x_ref, pool_ref, (o1_ref, o2_ref, o3_ref),
                                 (l1_ref, l2_ref, l3_ref), expand_ref, wout_ref, result)
        yield
        x1 = x_ref[0] + _rms_scale(result["mix"]) * gmix_ref[...]
        x1_stage[...] = x1
        h_stage[...] = (_rms_scale(x1) * gpre_ref[...]).astype(_BF16)

    _emit_interleaved(_gate_up_stages(h_prev, wg_ref, wu_ref, act_ref), stage_tile())
    f = jnp.dot(act_ref[...], wd_ref[...], preferred_element_type=_F32)
    out_ref[0] = x1_prev[...] + _rms_scale(f) * gpost_ref[...]


def _post(x, pool, outs, lses, expand, w_out, gmix, gpre, wg, wu, wd, gpost):
    b, s, _ = x.shape
    tm = POST_TILE
    per_seq = s // tm
    n_tiles = b * per_seq
    merged = lambda t: jnp.minimum(t, n_tiles - 1)
    written = lambda t: jnp.maximum(t - 1, 0)
    row = lambda t: (merged(t) // per_seq, merged(t) % per_seq, 0)
    slab = lambda t: (merged(t) // per_seq, 0, merged(t) % per_seq, 0)
    out_row = lambda t: (written(t) // per_seq, written(t) % per_seq, 0)
    return pl.pallas_call(
        _post_kernel,
        grid=(n_tiles + 1,),
        in_specs=[
            pl.BlockSpec((1, tm, D_MODEL), row),
            pl.BlockSpec((1, tm, POOL_WIDTH), row),
            pl.BlockSpec((1, N_PAIRS, tm, LANES), slab),
            pl.BlockSpec((1, N_PAIRS, tm, LANES), slab),
            pl.BlockSpec((1, N_PAIRS, tm, LANES), slab),
            pl.BlockSpec((1, PAIR_GROUPS, tm, LANES), slab),
            pl.BlockSpec((1, PAIR_GROUPS, tm, LANES), slab),
            pl.BlockSpec((1, PAIR_GROUPS, tm, LANES), slab),
            _resident((LANES, ATTN_WIDTH)),
            _resident((D_MODEL, D_MODEL)),
            _resident((1, D_MODEL)),
            _resident((1, D_MODEL)),
            _resident((D_MODEL, D_FF)),
            _resident((D_MODEL, D_FF)),
            _resident((D_FF, D_MODEL)),
            _resident((1, D_MODEL)),
        ],
        out_specs=pl.BlockSpec((1, tm, D_MODEL), out_row),
        out_shape=jax.ShapeDtypeStruct((b, s, D_MODEL), _F32),
        scratch_shapes=[
            pltpu.VMEM((tm, D_FF), _BF16),
            pltpu.VMEM((tm, D_MODEL), _F32),
            pltpu.VMEM((tm, D_MODEL), _BF16),
            pltpu.VMEM((tm, D_MODEL), _F32),
            pltpu.VMEM((tm, D_MODEL), _BF16),
        ],
        compiler_params=pltpu.CompilerParams(
            dimension_semantics=("arbitrary",), vmem_limit_bytes=VMEM_LIMIT),
        name="post",
    )(x, pool, *outs, *lses, expand, w_out, gmix, gpre, wg, wu, wd, gpost)


def _rope_tables(seq_len):
    freqs = ROPE_THETA ** (-jnp.arange(HALF_DIM, dtype=_F32) * (2.0 / HEAD_DIM))
    ang = jnp.arange(seq_len).astype(_F32)[:, None] * freqs[None, :]
    cos, sin = jnp.cos(ang), jnp.sin(ang)
    cos_t = jnp.concatenate([cos, cos, cos, cos], axis=1)
    sin_t = jnp.concatenate([-sin, sin, -sin, sin], axis=1)
    return cos_t, sin_t


def _block_diag(w_pool):
    n_g = w_pool.shape[0]
    out = jnp.zeros((POOL_WIDTH, POOL_WIDTH), w_pool.dtype)
    for gi in range(n_g):
        out = out.at[gi * POOL_GROUP:(gi + 1) * POOL_GROUP,
                     gi * POOL_GROUP:(gi + 1) * POOL_GROUP].set(w_pool[gi])
    return out


def _head_expand():
    e = np.zeros((LANES, ATTN_WIDTH), np.float32)
    for p in range(N_PAIRS):
        for hh in range(HEADS_PER_PAIR):
            h = HEADS_PER_PAIR * p + hh
            e[hh * HEAD_DIM + p, h * HEAD_DIM:(h + 1) * HEAD_DIM] = 1.0
    return jnp.asarray(e, dtype=_BF16)


def kernel(x, ln_pre_mix, w_in, w_pool, pool_scale, w_out, ln_post_mix, ln_pre_ffn,
           w_gate, w_up, w_down, ln_post_ffn):
    b, s, _ = x.shape
    depth = w_in.shape[0]
    cos_t, sin_t = _rope_tables(s)
    expand = _head_expand()
    for l in range(depth):
        pool, q, k, v = _in_proj(
            x, ln_pre_mix[l][None, :], w_in[l].astype(_BF16), cos_t, sin_t,
            _block_diag(w_pool[l]).astype(_BF16), pool_scale[l][None, :])
        outs, lses = [], []
        for _, dilation in DILATED_CONFIGS:
            o, lse = _attention_config(q, k, v, dilation)
            outs.append(o)
            lses.append(lse)
        x = _post(x, pool, outs, lses, expand, w_out[l].astype(_BF16), ln_post_mix[l][None, :],
                  ln_pre_ffn[l][None, :], w_gate[l].astype(_BF16), w_up[l].astype(_BF16),
                  w_down[l].astype(_BF16), ln_post_ffn[l][None, :])
    return x
```

```python
import functools
import math

import jax
import jax.numpy as jnp
import numpy as np
from jax import lax
from jax.experimental import pallas as pl
from jax.experimental.pallas import tpu as pltpu

D_MODEL = 1024
POOL_WIDTH = 256
POOL_WINDOWS = (2, 4, 8, 16)
POOL_GROUP = 64
HEAD_DIM = 64
HALF_DIM = HEAD_DIM // 2
ATTN_WIDTH = 768
N_HEADS = 12
DILATED_CONFIGS = ((128, 1), (512, 4), (2048, 16))
BLOCK = 128
ROPE_THETA = 10000.0
IN_WIDTH = POOL_WIDTH + 3 * ATTN_WIDTH
D_FF = 2816
EPS = 1e-6

LANES = 128
MAX_POOL_WIN = max(POOL_WINDOWS)

N_PAIRS = ATTN_WIDTH // LANES
HEADS_PER_PAIR = LANES // HEAD_DIM

TOKEN_TILE = 1024
IN_SUBTILES = 2
POST_TILE = 512
ATTN_CHUNK = 2048
PAIR_GROUPS = 2
PAIRS_PER_GROUP = N_PAIRS // PAIR_GROUPS
GATHER_STRIDE = 4
FF_CHUNK = 256
VMEM_LIMIT = 56 * 1024 * 1024

LOG2E = math.log2(math.e)
LN2 = math.log(2.0)

_BF16 = jnp.bfloat16
_F32 = jnp.float32


def _rms_scale(x):
    return x * lax.rsqrt(jnp.mean(x * x, axis=-1, keepdims=True) + EPS)


def _resident(shape):
    nd = len(shape)
    return pl.BlockSpec(shape, lambda *_: (0,) * nd, pipeline_mode=pl.Buffered(1))


def _emit_interleaved(main, side):
    live = [g for g in (main, side) if g is not None]
    while live:
        for g in list(live):
            if next(g, StopIteration) is StopIteration:
                live.remove(g)


def _rope_slab(xc, cos, sin_signed, first_half):
    ahead = pltpu.roll(xc, LANES - HALF_DIM, axis=1)
    behind = pltpu.roll(xc, HALF_DIM, axis=1)
    partner = jnp.where(first_half, ahead, behind)
    return xc * cos + partner * sin_signed


def _pool_mixer(u, first_pos, halo_ref, wpool_ref, pscale_ref):
    n = u.shape[0]
    ext = jnp.concatenate([halo_ref[...], u], axis=0)
    halo_ref[...] = u[n - MAX_POOL_WIN:, :]
    s2 = ext[1:, :] + ext[:-1, :]
    s4 = s2[2:, :] + s2[:-2, :]
    s8 = s4[4:, :] + s4[:-4, :]
    s16 = s8[8:, :] + s8[:-8, :]
    w2 = s2[MAX_POOL_WIN - 1:MAX_POOL_WIN - 1 + n, :]
    w4 = s4[MAX_POOL_WIN - 3:MAX_POOL_WIN - 3 + n, :]
    w8 = s8[MAX_POOL_WIN - 7:MAX_POOL_WIN - 7 + n, :]
    w16 = s16[1:1 + n, :]
    lane = lax.broadcasted_iota(jnp.int32, (n, POOL_WIDTH), 1)
    grp = lane // POOL_GROUP
    wsum = jnp.where(grp == 0, w2, jnp.where(grp == 1, w4, jnp.where(grp == 2, w8, w16)))
    win = jnp.where(grp == 0, 2, jnp.where(grp == 1, 4, jnp.where(grp == 2, 8, 16)))
    pos = first_pos + lax.broadcasted_iota(jnp.int32, (n, POOL_WIDTH), 0)
    cnt = jnp.minimum(pos + 1, win).astype(_F32)
    d = wsum / cnt - u
    y = jnp.dot(d.astype(_BF16), wpool_ref[...], preferred_element_type=_F32)
    return y * pscale_ref[...]


def _project_stages(rows, x_ref, g_ref, w_ref, proj):
    h = (_rms_scale(x_ref[0, rows, :]) * g_ref[...]).astype(_BF16)
    yield
    edges = (0, POOL_WIDTH, POOL_WIDTH + ATTN_WIDTH, POOL_WIDTH + 2 * ATTN_WIDTH, IN_WIDTH)
    for name, lo, hi in zip(("u", "q", "k", "v"), edges[:-1], edges[1:]):
        proj[name] = jnp.dot(h, w_ref[:, lo:hi], preferred_element_type=_F32)
        yield


def _finish_stages(rows, first_pos, proj, cos_ref, sin_ref, wpool_ref, pscale_ref, halo_ref,
                   pool_ref, q_ref, k_ref, v_ref):
    n = rows.stop - rows.start
    pool_ref[0, rows, :] = _pool_mixer(proj["u"], first_pos, halo_ref, wpool_ref,
                                       pscale_ref).astype(pool_ref.dtype)
    yield
    cos = cos_ref[rows, :]
    sin_signed = sin_ref[rows, :]
    lane128 = lax.broadcasted_iota(jnp.int32, (n, LANES), 1)
    first_half = (lane128 % HEAD_DIM) < HALF_DIM
    q_scale = (HEAD_DIM ** -0.5) * LOG2E
    for name, ref, scale in (("q", q_ref, q_scale), ("k", k_ref, None)):
        for p in range(N_PAIRS):
            t = _rope_slab(proj[name][:, p * LANES:(p + 1) * LANES], cos, sin_signed, first_half)
            ref[0, p, rows, :] = (t if scale is None else t * scale).astype(ref.dtype)
            if p % 2 == 1:
                yield
    for p in range(N_PAIRS):
        v_ref[0, p, rows, :] = proj["v"][:, p * LANES:(p + 1) * LANES].astype(v_ref.dtype)


def _in_proj_kernel(x_ref, g_ref, w_ref, cos_ref, sin_ref, wpool_ref, pscale_ref,
                    pool_ref, q_ref, k_ref, v_ref, halo_ref):
    i = pl.program_id(1)
    tm = x_ref.shape[1]
    sub = tm // IN_SUBTILES
    rows = [slice(s * sub, (s + 1) * sub) for s in range(IN_SUBTILES)]
    projs = [{} for _ in range(IN_SUBTILES)]

    @pl.when(i == 0)
    def _():
        halo_ref[...] = jnp.zeros_like(halo_ref)

    def project(s):
        return _project_stages(rows[s], x_ref, g_ref, w_ref, projs[s])

    def finish(s):
        return _finish_stages(rows[s], i * tm + s * sub, projs[s], cos_ref, sin_ref, wpool_ref,
                              pscale_ref, halo_ref, pool_ref, q_ref, k_ref, v_ref)

    _emit_interleaved(project(0), None)
    for s in range(IN_SUBTILES):
        _emit_interleaved(project(s + 1) if s + 1 < IN_SUBTILES else None, finish(s))


def _in_proj(x, g, w_in, cos_t, sin_t, wpool_bd, pscale):
    b, s, _ = x.shape
    tm = TOKEN_TILE
    row = lambda bi, i: (bi, i, 0)
    tab = lambda bi, i: (i, 0)
    slab = lambda bi, i: (bi, 0, i, 0)
    slab_shape = jax.ShapeDtypeStruct((b, N_PAIRS, s, LANES), _BF16)
    return pl.pallas_call(
        _in_proj_kernel,
        grid=(b, s // tm),
        in_specs=[
            pl.BlockSpec((1, tm, D_MODEL), row),
            _resident((1, D_MODEL)),
            _resident((D_MODEL, IN_WIDTH)),
            pl.BlockSpec((tm, LANES), tab),
            pl.BlockSpec((tm, LANES), tab),
            _resident((POOL_WIDTH, POOL_WIDTH)),
            _resident((1, POOL_WIDTH)),
        ],
        out_specs=(
            pl.BlockSpec((1, tm, POOL_WIDTH), row),
            pl.BlockSpec((1, N_PAIRS, tm, LANES), slab),
            pl.BlockSpec((1, N_PAIRS, tm, LANES), slab),
            pl.BlockSpec((1, N_PAIRS, tm, LANES), slab),
        ),
        out_shape=(jax.ShapeDtypeStruct((b, s, POOL_WIDTH), _BF16), slab_shape, slab_shape, slab_shape),
        scratch_shapes=[pltpu.VMEM((MAX_POOL_WIN, POOL_WIDTH), _F32)],
        compiler_params=pltpu.CompilerParams(
            dimension_semantics=("arbitrary", "arbitrary"), vmem_limit_bytes=VMEM_LIMIT),
        name="in_proj",
    )(x, g, w_in, cos_t, sin_t, wpool_bd, pscale)


def _blk_rows(blk):
    return slice(blk * BLOCK, (blk + 1) * BLOCK)


def _gather_streams(dilation, src, dst, st_a, st_b):
    chunk = src.shape[0]
    st_a[...] = src[...].astype(_F32)
    if dilation == GATHER_STRIDE:
        per_res = chunk // dilation
        for r in range(dilation):
            dst[r * per_res:(r + 1) * per_res, :] = (
                st_a[pl.ds(r, per_res, stride=GATHER_STRIDE), :].astype(dst.dtype))
    else:
        quarter = chunk // GATHER_STRIDE
        per_res = chunk // dilation
        for a in range(GATHER_STRIDE):
            st_b[a * quarter:(a + 1) * quarter, :] = st_a[pl.ds(a, quarter, stride=GATHER_STRIDE), :]
        for r in range(dilation):
            a, b = r % GATHER_STRIDE, r // GATHER_STRIDE
            dst[r * per_res:(r + 1) * per_res, :] = (
                st_b[pl.ds(a * quarter + b, per_res, stride=GATHER_STRIDE), :].astype(dst.dtype))


def _scatter_streams(dilation, src, dst, st_a, st_b):
    chunk = src.shape[0]
    if dilation == GATHER_STRIDE:
        per_res = chunk // dilation
        for r in range(dilation):
            st_a[pl.ds(r, per_res, stride=GATHER_STRIDE), :] = src[r * per_res:(r + 1) * per_res, :]
    else:
        quarter = chunk // GATHER_STRIDE
        per_res = chunk // dilation
        for r in range(dilation):
            a, b = r % GATHER_STRIDE, r // GATHER_STRIDE
            st_b[pl.ds(a * quarter + b, per_res, stride=GATHER_STRIDE), :] = (
                src[r * per_res:(r + 1) * per_res, :])
        for a in range(GATHER_STRIDE):
            st_a[pl.ds(a, quarter, stride=GATHER_STRIDE), :] = st_b[a * quarter:(a + 1) * quarter, :]
    dst[...] = st_a[...].astype(dst.dtype)


def _attn_kernel(dilation, q_ref, k_ref, v_ref, o_ref, lse_ref,
                 qcur, kcur, vcur, kprev, vprev, ocur, lcur, st_a, st_b, bias_ref):
    c = pl.program_id(2)
    chunk = q_ref.shape[2]
    n_pairs = q_ref.shape[1]
    per_res = chunk // (dilation * BLOCK)
    n_blocks = chunk // BLOCK
    strided = dilation > 1

    @pl.when(c == 0)
    def _():
        kprev[...] = jnp.zeros_like(kprev)
        vprev[...] = jnp.zeros_like(vprev)

    qi = lax.broadcasted_iota(jnp.int32, (BLOCK, 2 * BLOCK), 0)
    kj = lax.broadcasted_iota(jnp.int32, (BLOCK, 2 * BLOCK), 1)
    dist = qi + BLOCK - kj
    band = (dist >= 0) & (dist <= BLOCK)
    band_first = band & (kj >= BLOCK * (1 - jnp.minimum(c, 1)))
    bias_ref[0] = jnp.where(band_first, 0.0, -jnp.inf)
    bias_ref[1] = jnp.where(band, 0.0, -jnp.inf)
    lane = lax.broadcasted_iota(jnp.int32, (BLOCK, LANES), 1)
    low_head = lane < HEAD_DIM
    ones_block = jnp.ones((2 * BLOCK, LANES), _BF16)
    zero_bf16 = jnp.zeros((BLOCK, LANES), _BF16)

    def gather(p):
        for j, (src, dst) in enumerate(((q_ref, qcur), (k_ref, kcur), (v_ref, vcur))):
            _gather_streams(dilation, src.at[0, p], dst.at[p], st_a.at[j], st_b.at[j])

    def sources(p):
        if strided:
            return qcur.at[p], kcur.at[p], vcur.at[p], ocur.at[p], lcur
        return q_ref.at[0, p], k_ref.at[0, p], v_ref.at[0, p], o_ref.at[0, p], lse_ref.at[0, 0]

    def scores(p, blk):
        q_src, k_src = sources(p)[:2]
        r, i = divmod(blk, per_res)
        q = q_src[_blk_rows(blk), :]
        q2 = jnp.concatenate([jnp.where(low_head, q, zero_bf16),
                              jnp.where(low_head, zero_bf16, q)], axis=0)
        k_prev = kprev[p, _blk_rows(r), :] if i == 0 else k_src[_blk_rows(blk - 1), :]
        k_band = jnp.concatenate([k_prev, k_src[_blk_rows(blk), :]], axis=0)
        return lax.dot_general(q2, k_band, (((1,), (1,)), ((), ())),
                               preferred_element_type=_F32)

    def attend(p, blk, sc):
        _, _, v_src, o_dst, l_dst = sources(p)
        r, i = divmod(blk, per_res)
        bias = bias_ref.at[0 if i == 0 else 1]
        v_prev = vprev[p, _blk_rows(r), :] if i == 0 else v_src[_blk_rows(blk - 1), :]
        v_band = jnp.concatenate([v_prev, v_src[_blk_rows(blk), :]], axis=0)
        maxes, probs = [], []
        for hh in range(HEADS_PER_PAIR):
            s_h = sc[hh * BLOCK:(hh + 1) * BLOCK, :] + bias[...]
            m = jnp.max(s_h, axis=-1, keepdims=True)
            maxes.append(m)
            probs.append(jnp.exp2(s_h - m).astype(_BF16))
        v_aug = jnp.concatenate([v_band, ones_block], axis=1)
        pv = jnp.dot(jnp.concatenate(probs, axis=0), v_aug, preferred_element_type=_F32)
        num = jnp.where(low_head, pv[0:BLOCK, 0:LANES], pv[BLOCK:, 0:LANES])
        den = jnp.where(low_head, pv[0:BLOCK, LANES:], pv[BLOCK:, LANES:])
        o_dst[_blk_rows(blk), :] = (num * (1.0 / den)).astype(o_dst.dtype)
        lse = jnp.where(low_head, maxes[0], maxes[1]) * LN2 + jnp.log(den)
        head_lanes = (lane % HEAD_DIM) == p
        if p == 0:
            l_dst[_blk_rows(blk), :] = jnp.where(head_lanes, lse, 0.0)
        else:
            l_dst[_blk_rows(blk), :] = jnp.where(head_lanes, lse, l_dst[_blk_rows(blk), :])

    def carry(p):
        _, k_src, v_src = sources(p)[:3]
        for r in range(dilation):
            last = r * per_res + per_res - 1
            kprev[p, _blk_rows(r), :] = k_src[_blk_rows(last), :]
            vprev[p, _blk_rows(r), :] = v_src[_blk_rows(last), :]

    def scatter(p):
        _scatter_streams(dilation, ocur.at[p], o_ref.at[0, p], st_a.at[0], st_b.at[0])

    work = [(p, blk) for p in range(n_pairs) for blk in range(n_blocks)]
    if strided:
        gather(0)
    sc_next = None
    for n, (p, blk) in enumerate(work):
        if blk == 0 and strided and p + 1 < n_pairs:
            gather(p + 1)
        sc = scores(p, blk) if n == 0 else sc_next
        if n + 1 < len(work):
            sc_next = scores(*work[n + 1])
        attend(p, blk, sc)
        if blk == n_blocks - 1:
            carry(p)
            if strided and p >= 1:
                scatter(p - 1)
    if strided:
        scatter(n_pairs - 1)
        _scatter_streams(dilation, lcur, lse_ref.at[0, 0], st_a.at[1], st_b.at[1])


def _attention_config(q, k, v, dilation):
    b, _, s, _ = q.shape
    chunk = ATTN_CHUNK
    pg = PAIRS_PER_GROUP
    slab = lambda bi, g, c: (bi, g, c, 0)
    staged = chunk if dilation > 1 else 8
    staged_b = chunk if dilation > GATHER_STRIDE else 8
    return pl.pallas_call(
        functools.partial(_attn_kernel, dilation),
        grid=(b, PAIR_GROUPS, s // chunk),
        in_specs=[pl.BlockSpec((1, pg, chunk, LANES), slab)] * 3,
        out_specs=(
            pl.BlockSpec((1, pg, chunk, LANES), slab),
            pl.BlockSpec((1, 1, chunk, LANES), slab),
        ),
        out_shape=(
            jax.ShapeDtypeStruct((b, N_PAIRS, s, LANES), _BF16),
            jax.ShapeDtypeStruct((b, PAIR_GROUPS, s, LANES), _F32),
        ),
        scratch_shapes=[
            pltpu.VMEM((pg, staged, LANES), _BF16),
            pltpu.VMEM((pg, staged, LANES), _BF16),
            pltpu.VMEM((pg, staged, LANES), _BF16),
            pltpu.VMEM((pg, dilation * BLOCK, LANES), _BF16),
            pltpu.VMEM((pg, dilation * BLOCK, LANES), _BF16),
            pltpu.VMEM((pg, staged, LANES), _F32),
            pltpu.VMEM((staged, LANES), _F32),
            pltpu.VMEM((3, staged, LANES), _F32),
            pltpu.VMEM((3, staged_b, LANES), _F32),
            pltpu.VMEM((2, BLOCK, 2 * BLOCK), _F32),
        ],
        compiler_params=pltpu.CompilerParams(
            dimension_semantics=("arbitrary", "arbitrary", "arbitrary"),
            vmem_limit_bytes=VMEM_LIMIT),
        name=f"attn_d{dilation}",
    )(q, k, v)


def _merge_stages(rows, x_ref, pool_ref, o_refs, l_refs, expand_ref, w_ref, result):
    n_rows = rows.stop - rows.start
    lane = lax.broadcasted_iota(jnp.int32, (n_rows, LANES), 1)

    def head_lse(l_ref):
        return jnp.where((lane % HEAD_DIM) < PAIRS_PER_GROUP, l_ref[0, 0, rows, :],
                         pltpu.roll(l_ref[0, 1, rows, :], PAIRS_PER_GROUP, axis=1))

    l1, l2, l3 = (head_lse(l_ref) for l_ref in l_refs)
    mx = jnp.maximum(jnp.maximum(l1, l2), l3)
    e1, e2, e3 = jnp.exp(l1 - mx), jnp.exp(l2 - mx), jnp.exp(l3 - mx)
    inv = 1.0 / (e1 + e2 + e3)
    weights = [(e * inv).astype(_BF16) for e in (e1, e2, e3)]
    yield
    wexps = [jnp.dot(w, expand_ref[...], preferred_element_type=_F32) for w in weights]
    mix_pool = jnp.dot(pool_ref[0, rows, :], w_ref[0:POOL_WIDTH, :], preferred_element_type=_F32)
    yield
    attn = None
    for wexp, o_ref in zip(wexps, o_refs):
        o = jnp.concatenate([o_ref[0, p, rows, :] for p in range(N_PAIRS)], axis=1).astype(_F32)
        term = wexp * o
        attn = term if attn is None else attn + term
        yield
    result["mix"] = mix_pool + jnp.dot(attn.astype(_BF16), w_ref[POOL_WIDTH:, :],
                                       preferred_element_type=_F32)


def _gate_up_stages(h_ref, wg_ref, wu_ref, act_ref):
    for c in range(D_FF // FF_CHUNK):
        cols = slice(c * FF_CHUNK, (c + 1) * FF_CHUNK)
        gate = jnp.dot(h_ref[...], wg_ref[:, cols], preferred_element_type=_F32)
        up = jnp.dot(h_ref[...], wu_ref[:, cols], preferred_element_type=_F32)
        act_ref[:, cols] = (gate * jax.nn.sigmoid(gate) * up).astype(act_ref.dtype)
        yield


def _post_kernel(x_ref, pool_ref, o1_ref, o2_ref, o3_ref, l1_ref, l2_ref, l3_ref, expand_ref,
                 wout_ref, gmix_ref, gpre_ref, wg_ref, wu_ref, wd_ref, gpost_ref, out_ref,
                 act_ref, x1_stage, h_stage, x1_prev, h_prev):
    t = pl.program_id(0)
    tm = x_ref.shape[1]

    @pl.when(t == 0)
    def _():
        x1_stage[...] = jnp.zeros_like(x1_stage)
        h_stage[...] = jnp.zeros_like(h_stage)

    x1_prev[...] = x1_stage[...]
    h_prev[...] = h_stage[...]

    def stage_tile():
        result = {}
        yield from _merge_stages(slice(0, tm), x_ref, pool_ref, (o1_ref, o2_ref, o3_ref),
                                 (l1_ref, l2_ref, l3_ref), expand_ref, wout_ref, result)
        yield
        x1 = x_ref[0] + _rms_scale(result["mix"]) * gmix_ref[...]
        x1_stage[...] = x1
        h_stage[...] = (_rms_scale(x1) * gpre_ref[...]).astype(_BF16)

    _emit_interleaved(_gate_up_stages(h_prev, wg_ref, wu_ref, act_ref), stage_tile())
    f = jnp.dot(act_ref[...], wd_ref[...], preferred_element_type=_F32)
    out_ref[0] = x1_prev[...] + _rms_scale(f) * gpost_ref[...]


def _post(x, pool, outs, lses, expand, w_out, gmix, gpre, wg, wu, wd, gpost):
    b, s, _ = x.shape
    tm = POST_TILE
    per_seq = s // tm
    n_tiles = b * per_seq
    merged = lambda t: jnp.minimum(t, n_tiles - 1)
    written = lambda t: jnp.maximum(t - 1, 0)
    row = lambda t: (merged(t) // per_seq, merged(t) % per_seq, 0)
    slab = lambda t: (merged(t) // per_seq, 0, merged(t) % per_seq, 0)
    out_row = lambda t: (written(t) // per_seq, written(t) % per_seq, 0)
    return pl.pallas_call(
        _post_kernel,
        grid=(n_tiles + 1,),
        in_specs=[
            pl.BlockSpec((1, tm, D_MODEL), row),
            pl.BlockSpec((1, tm, POOL_WIDTH), row),
            pl.BlockSpec((1, N_PAIRS, tm, LANES), slab),
            pl.BlockSpec((1, N_PAIRS, tm, LANES), slab),
            pl.BlockSpec((1, N_PAIRS, tm, LANES), slab),
            pl.BlockSpec((1, PAIR_GROUPS, tm, LANES), slab),
            pl.BlockSpec((1, PAIR_GROUPS, tm, LANES), slab),
            pl.BlockSpec((1, PAIR_GROUPS, tm, LANES), slab),
            _resident((LANES, ATTN_WIDTH)),
            _resident((D_MODEL, D_MODEL)),
            _resident((1, D_MODEL)),
            _resident((1, D_MODEL)),
            _resident((D_MODEL, D_FF)),
            _resident((D_MODEL, D_FF)),
            _resident((D_FF, D_MODEL)),
            _resident((1, D_MODEL)),
        ],
        out_specs=pl.BlockSpec((1, tm, D_MODEL), out_row),
        out_shape=jax.ShapeDtypeStruct((b, s, D_MODEL), _F32),
        scratch_shapes=[
            pltpu.VMEM((tm, D_FF), _BF16),
            pltpu.VMEM((tm, D_MODEL), _F32),
            pltpu.VMEM((tm, D_MODEL), _BF16),
            pltpu.VMEM((tm, D_MODEL), _F32),
            pltpu.VMEM((tm, D_MODEL), _BF16),
        ],
        compiler_params=pltpu.CompilerParams(
            dimension_semantics=("arbitrary",), vmem_limit_bytes=VMEM_LIMIT),
        name="post",
    )(x, pool, *outs, *lses, expand, w_out, gmix, gpre, wg, wu, wd, gpost)


def _rope_tables(seq_len):
    freqs = ROPE_THETA ** (-jnp.arange(HALF_DIM, dtype=_F32) * (2.0 / HEAD_DIM))
    ang = jnp.arange(seq_len).astype(_F32)[:, None] * freqs[None, :]
    cos, sin = jnp.cos(ang), jnp.sin(ang)
    cos_t = jnp.concatenate([cos, cos, cos, cos], axis=1)
    sin_t = jnp.concatenate([-sin, sin, -sin, sin], axis=1)
    return cos_t, sin_t


def _block_diag(w_pool):
    n_g = w_pool.shape[0]
    out = jnp.zeros((POOL_WIDTH, POOL_WIDTH), w_pool.dtype)
    for gi in range(n_g):
        out = out.at[gi * POOL_GROUP:(gi + 1) * POOL_GROUP,
                     gi * POOL_GROUP:(gi + 1) * POOL_GROUP].set(w_pool[gi])
    return out


def _head_expand():
    e = np.zeros((LANES, ATTN_WIDTH), np.float32)
    for p in range(N_PAIRS):
        for hh in range(HEADS_PER_PAIR):
            h = HEADS_PER_PAIR * p + hh
            e[hh * HEAD_DIM + p, h * HEAD_DIM:(h + 1) * HEAD_DIM] = 1.0
    return jnp.asarray(e, dtype=_BF16)


def kernel(x, ln_pre_mix, w_in, w_pool, pool_scale, w_out, ln_post_mix, ln_pre_ffn,
           w_gate, w_up, w_down, ln_post_ffn):
    b, s, _ = x.shape
    depth = w_in.shape[0]
    cos_t, sin_t = _rope_tables(s)
    expand = _head_expand()
    for l in range(depth):
        pool, q, k, v = _in_proj(
            x, ln_pre_mix[l][None, :], w_in[l].astype(_BF16), cos_t, sin_t,
            _block_diag(w_pool[l]).astype(_BF16), pool_scale[l][None, :])
        outs, lses = [], []
        for _, dilation in DILATED_CONFIGS:
            o, lse = _attention_config(q, k, v, dilation)
            outs.append(o)
            lses.append(lse)
        x = _post(x, pool, outs, lses, expand, w_out[l].astype(_BF16), ln_post_mix[l][None, :],
                  ln_pre_ffn[l][None, :], w_gate[l].astype(_BF16), w_up[l].astype(_BF16),
                  w_down[l].astype(_BF16), ln_post_ffn[l][None, :])
    return x
```

```python
import functools
import math

import jax
import jax.numpy as jnp
import numpy as np
from jax import lax
from jax.experimental import pallas as pl
from jax.experimental.pallas import tpu as pltpu

D_MODEL = 1024
POOL_WIDTH = 256
POOL_WINDOWS = (2, 4, 8, 16)
POOL_GROUP = 64
HEAD_DIM = 64
HALF_DIM = HEAD_DIM // 2
ATTN_WIDTH = 768
N_HEADS = 12
DILATED_CONFIGS = ((128, 1), (512, 4), (2048, 16))
DILATIONS = tuple(d for _, d in DILATED_CONFIGS)
BLOCK = 128
ROPE_THETA = 10000.0
IN_WIDTH = POOL_WIDTH + 3 * ATTN_WIDTH
D_FF = 2816
EPS = 1e-6

LANES = 128
MAX_POOL_WIN = max(POOL_WINDOWS)

N_PAIRS = ATTN_WIDTH // LANES
HEADS_PER_PAIR = LANES // HEAD_DIM

TOKEN_TILE = 1024
IN_SUBTILES = 2
POST_TILE = 512
ATTN_CHUNK = 2048
PAIR_GROUPS = 2
PAIRS_PER_GROUP = N_PAIRS // PAIR_GROUPS
FAST_STRIDE = 4
N_STAGING = 4
FF_CHUNK = 256
VMEM_LIMIT = 56 * 1024 * 1024

LOG2E = math.log2(math.e)
LN2 = math.log(2.0)

_BF16 = jnp.bfloat16
_F32 = jnp.float32


def _rms_scale(x):
    return x * lax.rsqrt(jnp.mean(x * x, axis=-1, keepdims=True) + EPS)


def _resident(shape):
    nd = len(shape)
    return pl.BlockSpec(shape, lambda *_: (0,) * nd, pipeline_mode=pl.Buffered(1))


def _emit_interleaved(main, side):
    live = [g for g in (main, side) if g is not None]
    while live:
        for g in list(live):
            if next(g, StopIteration) is StopIteration:
                live.remove(g)


def _stream_shape(batch, lead, seq_len, dilation, dtype):
    return jax.ShapeDtypeStruct((batch, lead, dilation, seq_len // dilation, LANES), dtype)


def _rope_slab(xc, cos, sin_signed, first_half):
    ahead = pltpu.roll(xc, LANES - HALF_DIM, axis=1)
    behind = pltpu.roll(xc, HALF_DIM, axis=1)
    partner = jnp.where(first_half, ahead, behind)
    return xc * cos + partner * sin_signed


def _pool_mixer(u, first_pos, halo_ref, wpool_ref, pscale_ref):
    n = u.shape[0]
    ext = jnp.concatenate([halo_ref[...], u], axis=0)
    halo_ref[...] = u[n - MAX_POOL_WIN:, :]
    s2 = ext[1:, :] + ext[:-1, :]
    s4 = s2[2:, :] + s2[:-2, :]
    s8 = s4[4:, :] + s4[:-4, :]
    s16 = s8[8:, :] + s8[:-8, :]
    w2 = s2[MAX_POOL_WIN - 1:MAX_POOL_WIN - 1 + n, :]
    w4 = s4[MAX_POOL_WIN - 3:MAX_POOL_WIN - 3 + n, :]
    w8 = s8[MAX_POOL_WIN - 7:MAX_POOL_WIN - 7 + n, :]
    w16 = s16[1:1 + n, :]
    lane = lax.broadcasted_iota(jnp.int32, (n, POOL_WIDTH), 1)
    grp = lane // POOL_GROUP
    wsum = jnp.where(grp == 0, w2, jnp.where(grp == 1, w4, jnp.where(grp == 2, w8, w16)))
    win = jnp.where(grp == 0, 2, jnp.where(grp == 1, 4, jnp.where(grp == 2, 8, 16)))
    pos = first_pos + lax.broadcasted_iota(jnp.int32, (n, POOL_WIDTH), 0)
    cnt = jnp.minimum(pos + 1, win).astype(_F32)
    d = wsum / cnt - u
    y = jnp.dot(d.astype(_BF16), wpool_ref[...], preferred_element_type=_F32)
    return y * pscale_ref[...]


def _store_streams(t, st_a, st_b, out_refs, p, s, n):
    ref1, ref4, ref16 = out_refs
    ref1[0, p, 0, s * n:(s + 1) * n, :] = t.astype(ref1.dtype)
    st_a[...] = t
    quarter = n // FAST_STRIDE
    for a in range(FAST_STRIDE):
        rows_a = st_a[pl.ds(a, quarter, stride=FAST_STRIDE), :]
        ref4[0, p, a, s * quarter:(s + 1) * quarter, :] = rows_a.astype(ref4.dtype)
        st_b[a * quarter:(a + 1) * quarter, :] = rows_a
    per_res = n // DILATIONS[2]
    for r in range(DILATIONS[2]):
        a, b = r % FAST_STRIDE, r // FAST_STRIDE
        ref16[0, p, r, s * per_res:(s + 1) * per_res, :] = (
            st_b[pl.ds(a * quarter + b, per_res, stride=FAST_STRIDE), :].astype(ref16.dtype))


def _project_stages(rows, x_ref, g_ref, w_ref, proj):
    h = (_rms_scale(x_ref[0, rows, :]) * g_ref[...]).astype(_BF16)
    yield
    edges = (0, POOL_WIDTH, POOL_WIDTH + ATTN_WIDTH, POOL_WIDTH + 2 * ATTN_WIDTH, IN_WIDTH)
    for name, lo, hi in zip(("u", "q", "k", "v"), edges[:-1], edges[1:]):
        proj[name] = jnp.dot(h, w_ref[:, lo:hi], preferred_element_type=_F32)
        yield


def _finish_stages(s, rows, first_pos, proj, cos_ref, sin_ref, wpool_ref, pscale_ref, halo_ref,
                   pool_ref, q_refs, k_refs, v_refs, staging):
    n = rows.stop - rows.start
    pool_ref[0, rows, :] = _pool_mixer(proj["u"], first_pos, halo_ref, wpool_ref,
                                       pscale_ref).astype(pool_ref.dtype)
    yield
    cos = cos_ref[rows, :]
    sin_signed = sin_ref[rows, :]
    lane128 = lax.broadcasted_iota(jnp.int32, (n, LANES), 1)
    first_half = (lane128 % HEAD_DIM) < HALF_DIM
    q_scale = (HEAD_DIM ** -0.5) * LOG2E
    slab_count = 0
    for name, refs, scale in (("q", q_refs, q_scale), ("k", k_refs, None), ("v", v_refs, None)):
        for p in range(N_PAIRS):
            t = proj[name][:, p * LANES:(p + 1) * LANES]
            if name != "v":
                t = _rope_slab(t, cos, sin_signed, first_half)
            if scale is not None:
                t = t * scale
            j = slab_count % (N_STAGING // 2)
            _store_streams(t, staging.at[2 * j], staging.at[2 * j + 1], refs, p, s, n)
            slab_count += 1
            if p % 2 == 1:
                yield


def _in_proj_kernel(x_ref, g_ref, w_ref, cos_ref, sin_ref, wpool_ref, pscale_ref, pool_ref, *rest):
    n_dil = len(DILATIONS)
    q_refs, k_refs, v_refs = rest[0:n_dil], rest[n_dil:2 * n_dil], rest[2 * n_dil:3 * n_dil]
    halo_ref, staging = rest[3 * n_dil:]
    i = pl.program_id(1)
    tm = x_ref.shape[1]
    sub = tm // IN_SUBTILES
    rows = [slice(s * sub, (s + 1) * sub) for s in range(IN_SUBTILES)]
    projs = [{} for _ in range(IN_SUBTILES)]

    @pl.when(i == 0)
    def _():
        halo_ref[...] = jnp.zeros_like(halo_ref)

    def project(s):
        return _project_stages(rows[s], x_ref, g_ref, w_ref, projs[s])

    def finish(s):
        return _finish_stages(s, rows[s], i * tm + s * sub, projs[s], cos_ref, sin_ref, wpool_ref,
                              pscale_ref, halo_ref, pool_ref, q_refs, k_refs, v_refs, staging)

    _emit_interleaved(project(0), None)
    for s in range(IN_SUBTILES):
        _emit_interleaved(project(s + 1) if s + 1 < IN_SUBTILES else None, finish(s))


def _in_proj(x, g, w_in, cos_t, sin_t, wpool_bd, pscale):
    b, s, _ = x.shape
    tm = TOKEN_TILE
    row = lambda bi, i: (bi, i, 0)
    tab = lambda bi, i: (i, 0)
    stream = lambda bi, i: (bi, 0, 0, i, 0)
    stream_specs = [pl.BlockSpec((1, N_PAIRS, d, tm // d, LANES), stream) for d in DILATIONS] * 3
    stream_shapes = [_stream_shape(b, N_PAIRS, s, d, _BF16) for d in DILATIONS] * 3
    outs = pl.pallas_call(
        _in_proj_kernel,
        grid=(b, s // tm),
        in_specs=[
            pl.BlockSpec((1, tm, D_MODEL), row),
            _resident((1, D_MODEL)),
            _resident((D_MODEL, IN_WIDTH)),
            pl.BlockSpec((tm, LANES), tab),
            pl.BlockSpec((tm, LANES), tab),
            _resident((POOL_WIDTH, POOL_WIDTH)),
            _resident((1, POOL_WIDTH)),
        ],
        out_specs=[pl.BlockSpec((1, tm, POOL_WIDTH), row)] + stream_specs,
        out_shape=[jax.ShapeDtypeStruct((b, s, POOL_WIDTH), _BF16)] + stream_shapes,
        scratch_shapes=[
            pltpu.VMEM((MAX_POOL_WIN, POOL_WIDTH), _F32),
            pltpu.VMEM((N_STAGING, tm // IN_SUBTILES, LANES), _F32),
        ],
        compiler_params=pltpu.CompilerParams(
            dimension_semantics=("arbitrary", "arbitrary"), vmem_limit_bytes=VMEM_LIMIT),
        name="in_proj",
    )(x, g, w_in, cos_t, sin_t, wpool_bd, pscale)
    n_dil = len(DILATIONS)
    return outs[0], outs[1:1 + n_dil], outs[1 + n_dil:1 + 2 * n_dil], outs[1 + 2 * n_dil:]


def _blk_rows(blk):
    return slice(blk * BLOCK, (blk + 1) * BLOCK)


def _attn_kernel(q_ref, k_ref, v_ref, o_ref, lse_ref, kprev, vprev, bias_ref):
    c = pl.program_id(2)
    n_pairs, dilation, rows_per_res = q_ref.shape[1:4]
    per_res = rows_per_res // BLOCK

    @pl.when(c == 0)
    def _():
        kprev[...] = jnp.zeros_like(kprev)
        vprev[...] = jnp.zeros_like(vprev)

    qi = lax.broadcasted_iota(jnp.int32, (BLOCK, 2 * BLOCK), 0)
    kj = lax.broadcasted_iota(jnp.int32, (BLOCK, 2 * BLOCK), 1)
    dist = qi + BLOCK - kj
    band = (dist >= 0) & (dist <= BLOCK)
    band_first = band & (kj >= BLOCK * (1 - jnp.minimum(c, 1)))
    bias_ref[0] = jnp.where(band_first, 0.0, -jnp.inf)
    bias_ref[1] = jnp.where(band, 0.0, -jnp.inf)
    lane = lax.broadcasted_iota(jnp.int32, (BLOCK, LANES), 1)
    low_head = lane < HEAD_DIM
    ones_block = jnp.ones((2 * BLOCK, LANES), _BF16)
    zero_bf16 = jnp.zeros((BLOCK, LANES), _BF16)

    def band_of(ref, prev, p, r, i):
        head = prev[p, _blk_rows(r), :] if i == 0 else ref[0, p, r, _blk_rows(i - 1), :]
        return jnp.concatenate([head, ref[0, p, r, _blk_rows(i), :]], axis=0)

    def scores(p, r, i):
        q = q_ref[0, p, r, _blk_rows(i), :]
        q2 = jnp.concatenate([jnp.where(low_head, q, zero_bf16),
                              jnp.where(low_head, zero_bf16, q)], axis=0)
        return lax.dot_general(q2, band_of(k_ref, kprev, p, r, i), (((1,), (1,)), ((), ())),
                               preferred_element_type=_F32)

    def attend(p, r, i, sc):
        bias = bias_ref.at[0 if i == 0 else 1]
        maxes, probs = [], []
        for hh in range(HEADS_PER_PAIR):
            s_h = sc[hh * BLOCK:(hh + 1) * BLOCK, :] + bias[...]
            m = jnp.max(s_h, axis=-1, keepdims=True)
            maxes.append(m)
            probs.append(jnp.exp2(s_h - m).astype(_BF16))
        v_aug = jnp.concatenate([band_of(v_ref, vprev, p, r, i), ones_block], axis=1)
        pv = jnp.dot(jnp.concatenate(probs, axis=0), v_aug, preferred_element_type=_F32)
        num = jnp.where(low_head, pv[0:BLOCK, 0:LANES], pv[BLOCK:, 0:LANES])
        den = jnp.where(low_head, pv[0:BLOCK, LANES:], pv[BLOCK:, LANES:])
        o_ref[0, p, r, _blk_rows(i), :] = (num * (1.0 / den)).astype(o_ref.dtype)
        lse = jnp.where(low_head, maxes[0], maxes[1]) * LN2 + jnp.log(den)
        head_lanes = (lane % HEAD_DIM) == p
        if p == 0:
            lse_ref[0, 0, r, _blk_rows(i), :] = jnp.where(head_lanes, lse, 0.0)
        else:
            lse_ref[0, 0, r, _blk_rows(i), :] = jnp.where(head_lanes, lse,
                                                          lse_ref[0, 0, r, _blk_rows(i), :])

    work = [(p, r, i) for p in range(n_pairs) for r in range(dilation) for i in range(per_res)]
    sc_next = scores(*work[0])
    for n, item in enumerate(work):
        sc = sc_next
        if n + 1 < len(work):
            sc_next = scores(*work[n + 1])
        attend(*item, sc)
    for p in range(n_pairs):
        for r in range(dilation):
            kprev[p, _blk_rows(r), :] = k_ref[0, p, r, _blk_rows(per_res - 1), :]
            vprev[p, _blk_rows(r), :] = v_ref[0, p, r, _blk_rows(per_res - 1), :]


def _attention_config(q, k, v):
    b, _, dilation, length, _ = q.shape
    rows = ATTN_CHUNK // dilation
    pg = PAIRS_PER_GROUP
    stream = lambda bi, g, c: (bi, g, 0, c, 0)
    return pl.pallas_call(
        _attn_kernel,
        grid=(b, PAIR_GROUPS, length // rows),
        in_specs=[pl.BlockSpec((1, pg, dilation, rows, LANES), stream)] * 3,
        out_specs=(
            pl.BlockSpec((1, pg, dilation, rows, LANES), stream),
            pl.BlockSpec((1, 1, dilation, rows, LANES), stream),
        ),
        out_shape=(
            _stream_shape(b, N_PAIRS, dilation * length, dilation, _BF16),
            _stream_shape(b, PAIR_GROUPS, dilation * length, dilation, _F32),
        ),
        scratch_shapes=[
            pltpu.VMEM((pg, dilation * BLOCK, LANES), _BF16),
            pltpu.VMEM((pg, dilation * BLOCK, LANES), _BF16),
            pltpu.VMEM((2, BLOCK, 2 * BLOCK), _F32),
        ],
        compiler_params=pltpu.CompilerParams(
            dimension_semantics=("arbitrary", "arbitrary", "arbitrary"),
            vmem_limit_bytes=VMEM_LIMIT),
        name=f"attn_d{dilation}",
    )(q, k, v)


def _position_order(slab_ref, st_a, st_b):
    dilation, per_res, _ = slab_ref.shape
    if dilation == 1:
        return slab_ref[0].astype(_F32)
    if dilation == FAST_STRIDE:
        for r in range(dilation):
            st_a[pl.ds(r, per_res, stride=FAST_STRIDE), :] = slab_ref[r].astype(_F32)
        return st_a[...]
    quarter = dilation * per_res // FAST_STRIDE
    for r in range(dilation):
        a, b = r % FAST_STRIDE, r // FAST_STRIDE
        st_b[pl.ds(a * quarter + b, per_res, stride=FAST_STRIDE), :] = slab_ref[r].astype(_F32)
    for a in range(FAST_STRIDE):
        st_a[pl.ds(a, quarter, stride=FAST_STRIDE), :] = st_b[a * quarter:(a + 1) * quarter, :]
    return st_a[...]


def _merge_stages(x_ref, pool_ref, o_refs, l_refs, expand_ref, w_ref, staging, result):
    tm = x_ref.shape[1]
    lane = lax.broadcasted_iota(jnp.int32, (tm, LANES), 1)
    buffers = iter(range(10 ** 6))

    def ordered(slab_ref):
        j = next(buffers) % (N_STAGING // 2)
        return _position_order(slab_ref, staging.at[2 * j], staging.at[2 * j + 1])

    def head_lse(l_ref):
        return jnp.where((lane % HEAD_DIM) < PAIRS_PER_GROUP, ordered(l_ref.at[0, 0]),
                         pltpu.roll(ordered(l_ref.at[0, 1]), PAIRS_PER_GROUP, axis=1))

    l1, l2, l3 = (head_lse(l_ref) for l_ref in l_refs)
    mx = jnp.maximum(jnp.maximum(l1, l2), l3)
    e1, e2, e3 = jnp.exp(l1 - mx), jnp.exp(l2 - mx), jnp.exp(l3 - mx)
    inv = 1.0 / (e1 + e2 + e3)
    weights = [(e * inv).astype(_BF16) for e in (e1, e2, e3)]
    yield
    wexps = [jnp.dot(w, expand_ref[...], preferred_element_type=_F32) for w in weights]
    mix_pool = jnp.dot(pool_ref[0], w_ref[0:POOL_WIDTH, :], preferred_element_type=_F32)
    yield
    attn = None
    for wexp, o_ref in zip(wexps, o_refs):
        o = jnp.concatenate([ordered(o_ref.at[0, p]) for p in range(N_PAIRS)], axis=1)
        term = wexp * o
        attn = term if attn is None else attn + term
        yield
    result["mix"] = mix_pool + jnp.dot(attn.astype(_BF16), w_ref[POOL_WIDTH:, :],
                                       preferred_element_type=_F32)


def _gate_up_stages(h_ref, wg_ref, wu_ref, act_ref):
    for c in range(D_FF // FF_CHUNK):
        cols = slice(c * FF_CHUNK, (c + 1) * FF_CHUNK)
        gate = jnp.dot(h_ref[...], wg_ref[:, cols], preferred_element_type=_F32)
        up = jnp.dot(h_ref[...], wu_ref[:, cols], preferred_element_type=_F32)
        act_ref[:, cols] = (gate * jax.nn.sigmoid(gate) * up).astype(act_ref.dtype)
        yield


def _post_kernel(x_ref, pool_ref, o1_ref, o2_ref, o3_ref, l1_ref, l2_ref, l3_ref, expand_ref,
                 wout_ref, gmix_ref, gpre_ref, wg_ref, wu_ref, wd_ref, gpost_ref, out_ref,
                 act_ref, x1_stage, h_stage, x1_prev, h_prev, staging):
    t = pl.program_id(0)

    @pl.when(t == 0)
    def _():
        x1_stage[...] = jnp.zeros_like(x1_stage)
        h_stage[...] = jnp.zeros_like(h_stage)

    x1_prev[...] = x1_stage[...]
    h_prev[...] = h_stage[...]

    def stage_tile():
        result = {}
        yield from _merge_stages(x_ref, pool_ref, (o1_ref, o2_ref, o3_ref), (l1_ref, l2_ref, l3_ref),
                                 expand_ref, wout_ref, staging, result)
        yield
        x1 = x_ref[0] + _rms_scale(result["mix"]) * gmix_ref[...]
        x1_stage[...] = x1
        h_stage[...] = (_rms_scale(x1) * gpre_ref[...]).astype(_BF16)

    _emit_interleaved(_gate_up_stages(h_prev, wg_ref, wu_ref, act_ref), stage_tile())
    f = jnp.dot(act_ref[...], wd_ref[...], preferred_element_type=_F32)
    out_ref[0] = x1_prev[...] + _rms_scale(f) * gpost_ref[...]


def _post(x, pool, outs, lses, expand, w_out, gmix, gpre, wg, wu, wd, gpost):
    b, s, _ = x.shape
    tm = POST_TILE
    per_seq = s // tm
    n_tiles = b * per_seq
    merged = lambda t: jnp.minimum(t, n_tiles - 1)
    written = lambda t: jnp.maximum(t - 1, 0)
    row = lambda t: (merged(t) // per_seq, merged(t) % per_seq, 0)
    stream = lambda t: (merged(t) // per_seq, 0, 0, merged(t) % per_seq, 0)
    out_row = lambda t: (written(t) // per_seq, written(t) % per_seq, 0)
    return pl.pallas_call(
        _post_kernel,
        grid=(n_tiles + 1,),
        in_specs=[
            pl.BlockSpec((1, tm, D_MODEL), row),
            pl.BlockSpec((1, tm, POOL_WIDTH), row),
        ] + [pl.BlockSpec((1, N_PAIRS, d, tm // d, LANES), stream) for d in DILATIONS]
          + [pl.BlockSpec((1, PAIR_GROUPS, d, tm // d, LANES), stream) for d in DILATIONS] + [
            _resident((LANES, ATTN_WIDTH)),
            _resident((D_MODEL, D_MODEL)),
            _resident((1, D_MODEL)),
            _resident((1, D_MODEL)),
            _resident((D_MODEL, D_FF)),
            _resident((D_MODEL, D_FF)),
            _resident((D_FF, D_MODEL)),
            _resident((1, D_MODEL)),
        ],
        out_specs=pl.BlockSpec((1, tm, D_MODEL), out_row),
        out_shape=jax.ShapeDtypeStruct((b, s, D_MODEL), _F32),
        scratch_shapes=[
            pltpu.VMEM((tm, D_FF), _BF16),
            pltpu.VMEM((tm, D_MODEL), _F32),
            pltpu.VMEM((tm, D_MODEL), _BF16),
            pltpu.VMEM((tm, D_MODEL), _F32),
            pltpu.VMEM((tm, D_MODEL), _BF16),
            pltpu.VMEM((N_STAGING, tm, LANES), _F32),
        ],
        compiler_params=pltpu.CompilerParams(
            dimension_semantics=("arbitrary",), vmem_limit_bytes=VMEM_LIMIT),
        name="post",
    )(x, pool, *outs, *lses, expand, w_out, gmix, gpre, wg, wu, wd, gpost)


def _rope_tables(seq_len):
    freqs = ROPE_THETA ** (-jnp.arange(HALF_DIM, dtype=_F32) * (2.0 / HEAD_DIM))
    ang = jnp.arange(seq_len).astype(_F32)[:, None] * freqs[None, :]
    cos, sin = jnp.cos(ang), jnp.sin(ang)
    cos_t = jnp.concatenate([cos, cos, cos, cos], axis=1)
    sin_t = jnp.concatenate([-sin, sin, -sin, sin], axis=1)
    return cos_t, sin_t


def _block_diag(w_pool):
    n_g = w_pool.shape[0]
    out = jnp.zeros((POOL_WIDTH, POOL_WIDTH), w_pool.dtype)
    for gi in range(n_g):
        out = out.at[gi * POOL_GROUP:(gi + 1) * POOL_GROUP,
                     gi * POOL_GROUP:(gi + 1) * POOL_GROUP].set(w_pool[gi])
    return out


def _head_expand():
    e = np.zeros((LANES, ATTN_WIDTH), np.float32)
    for p in range(N_PAIRS):
        for hh in range(HEADS_PER_PAIR):
            h = HEADS_PER_PAIR * p + hh
            e[hh * HEAD_DIM + p, h * HEAD_DIM:(h + 1) * HEAD_DIM] = 1.0
    return jnp.asarray(e, dtype=_BF16)


def kernel(x, ln_pre_mix, w_in, w_pool, pool_scale, w_out, ln_post_mix, ln_pre_ffn,
           w_gate, w_up, w_down, ln_post_ffn):
    b, s, _ = x.shape
    depth = w_in.shape[0]
    cos_t, sin_t = _rope_tables(s)
    expand = _head_expand()
    for l in range(depth):
        pool, qs, ks, vs = _in_proj(
            x, ln_pre_mix[l][None, :], w_in[l].astype(_BF16), cos_t, sin_t,
            _block_diag(w_pool[l]).astype(_BF16), pool_scale[l][None, :])
        outs, lses = [], []
        for q, k, v in zip(qs, ks, vs):
            o, lse = _attention_config(q, k, v)
            outs.append(o)
            lses.append(lse)
        x = _post(x, pool, outs, lses, expand, w_out[l].astype(_BF16), ln_post_mix[l][None, :],
                  ln_pre_ffn[l][None, :], w_gate[l].astype(_BF16), w_up[l].astype(_BF16),
                  w_down[l].astype(_BF16), ln_post_ffn[l][None, :])
    return x
```

```python
import functools
import math

import jax
import jax.numpy as jnp
import numpy as np
from jax import lax
from jax.experimental import pallas as pl
from jax.experimental.pallas import tpu as pltpu

D_MODEL = 1024
POOL_WIDTH = 256
POOL_WINDOWS = (2, 4, 8, 16)
POOL_GROUP = 64
HEAD_DIM = 64
HALF_DIM = HEAD_DIM // 2
ATTN_WIDTH = 768
N_HEADS = 12
DILATED_CONFIGS = ((128, 1), (512, 4), (2048, 16))
DILATIONS = tuple(d for _, d in DILATED_CONFIGS)
BLOCK = 128
ROPE_THETA = 10000.0
IN_WIDTH = POOL_WIDTH + 3 * ATTN_WIDTH
D_FF = 2816
EPS = 1e-6

LANES = 128
MAX_POOL_WIN = max(POOL_WINDOWS)

N_PAIRS = ATTN_WIDTH // LANES
HEADS_PER_PAIR = LANES // HEAD_DIM

TOKEN_TILE = 1024
IN_SUBTILES = 4
POST_TILE = 512
ATTN_CHUNK = 2048
PAIR_GROUPS = 2
PAIRS_PER_GROUP = N_PAIRS // PAIR_GROUPS
FAST_STRIDE = 4
N_STAGING = 12
FF_CHUNK = 256
H_EARLY_CHUNKS = 3
VMEM_LIMIT = 56 * 1024 * 1024

LOG2E = math.log2(math.e)
LN2 = math.log(2.0)

_BF16 = jnp.bfloat16
_F32 = jnp.float32


def _rms_scale(x):
    return x * lax.rsqrt(jnp.mean(x * x, axis=-1, keepdims=True) + EPS)


def _resident(shape):
    nd = len(shape)
    return pl.BlockSpec(shape, lambda *_: (0,) * nd, pipeline_mode=pl.Buffered(1))


def _emit_interleaved(main, side):
    live = [g for g in (main, side) if g is not None]
    while live:
        for g in list(live):
            if next(g, StopIteration) is StopIteration:
                live.remove(g)


def _stream_shape(batch, lead, seq_len, dilation, dtype):
    return jax.ShapeDtypeStruct((batch, lead, dilation, seq_len // dilation, LANES), dtype)


def _rope_slab(xc, cos, sin_signed, first_half):
    ahead = pltpu.roll(xc, LANES - HALF_DIM, axis=1)
    behind = pltpu.roll(xc, HALF_DIM, axis=1)
    partner = jnp.where(first_half, ahead, behind)
    return xc * cos + partner * sin_signed


def _pool_mixer(u, first_pos, halo_ref, wpool_ref, pscale_ref):
    n = u.shape[0]
    ext = jnp.concatenate([halo_ref[...], u], axis=0)
    halo_ref[...] = u[n - MAX_POOL_WIN:, :]
    s2 = ext[1:, :] + ext[:-1, :]
    s4 = s2[2:, :] + s2[:-2, :]
    s8 = s4[4:, :] + s4[:-4, :]
    s16 = s8[8:, :] + s8[:-8, :]
    w2 = s2[MAX_POOL_WIN - 1:MAX_POOL_WIN - 1 + n, :]
    w4 = s4[MAX_POOL_WIN - 3:MAX_POOL_WIN - 3 + n, :]
    w8 = s8[MAX_POOL_WIN - 7:MAX_POOL_WIN - 7 + n, :]
    w16 = s16[1:1 + n, :]
    lane = lax.broadcasted_iota(jnp.int32, (n, POOL_WIDTH), 1)
    grp = lane // POOL_GROUP
    wsum = jnp.where(grp == 0, w2, jnp.where(grp == 1, w4, jnp.where(grp == 2, w8, w16)))
    win = jnp.where(grp == 0, 2, jnp.where(grp == 1, 4, jnp.where(grp == 2, 8, 16)))
    pos = first_pos + lax.broadcasted_iota(jnp.int32, (n, POOL_WIDTH), 0)
    cnt = jnp.minimum(pos + 1, win).astype(_F32)
    d = wsum / cnt - u
    y = jnp.dot(d.astype(_BF16), wpool_ref[...], preferred_element_type=_F32)
    return y * pscale_ref[...]


def _store_streams(t, st_a, st_b, out_refs, p, s, n):
    ref1, ref4, ref16 = out_refs
    ref1[0, p, 0, s * n:(s + 1) * n, :] = t.astype(ref1.dtype)
    st_a[...] = t
    quarter = n // FAST_STRIDE
    for a in range(FAST_STRIDE):
        rows_a = st_a[pl.ds(a, quarter, stride=FAST_STRIDE), :]
        ref4[0, p, a, s * quarter:(s + 1) * quarter, :] = rows_a.astype(ref4.dtype)
        st_b[a * quarter:(a + 1) * quarter, :] = rows_a
    per_res = n // DILATIONS[2]
    for r in range(DILATIONS[2]):
        a, b = r % FAST_STRIDE, r // FAST_STRIDE
        ref16[0, p, r, s * per_res:(s + 1) * per_res, :] = (
            st_b[pl.ds(a * quarter + b, per_res, stride=FAST_STRIDE), :].astype(ref16.dtype))


def _project_stages(rows, x_ref, g_ref, w_ref, proj):
    h = (_rms_scale(x_ref[0, rows, :]) * g_ref[...]).astype(_BF16)
    yield
    edges = (0, POOL_WIDTH, POOL_WIDTH + ATTN_WIDTH, POOL_WIDTH + 2 * ATTN_WIDTH, IN_WIDTH)
    for name, lo, hi in zip(("u", "q", "k", "v"), edges[:-1], edges[1:]):
        proj[name] = jnp.dot(h, w_ref[:, lo:hi], preferred_element_type=_F32)
        yield


def _finish_stages(s, rows, first_pos, proj, cos_ref, sin_ref, wpool_ref, pscale_ref, halo_ref,
                   pool_ref, q_refs, k_refs, v_refs, staging):
    n = rows.stop - rows.start
    pool_ref[0, rows, :] = _pool_mixer(proj["u"], first_pos, halo_ref, wpool_ref,
                                       pscale_ref).astype(pool_ref.dtype)
    yield
    cos = cos_ref[rows, :]
    sin_signed = sin_ref[rows, :]
    lane128 = lax.broadcasted_iota(jnp.int32, (n, LANES), 1)
    first_half = (lane128 % HEAD_DIM) < HALF_DIM
    q_scale = (HEAD_DIM ** -0.5) * LOG2E
    slab_count = 0
    for name, refs, scale in (("q", q_refs, q_scale), ("k", k_refs, None), ("v", v_refs, None)):
        for p in range(N_PAIRS):
            t = proj[name][:, p * LANES:(p + 1) * LANES]
            if name != "v":
                t = _rope_slab(t, cos, sin_signed, first_half)
            if scale is not None:
                t = t * scale
            j = slab_count % (N_STAGING // 2)
            _store_streams(t, staging.at[2 * j], staging.at[2 * j + 1], refs, p, s, n)
            slab_count += 1
            if p % 2 == 1:
                yield


def _in_proj_kernel(x_ref, g_ref, w_ref, cos_ref, sin_ref, wpool_ref, pscale_ref, pool_ref, *rest):
    n_dil = len(DILATIONS)
    q_refs, k_refs, v_refs = rest[0:n_dil], rest[n_dil:2 * n_dil], rest[2 * n_dil:3 * n_dil]
    halo_ref, staging = rest[3 * n_dil:]
    i = pl.program_id(1)
    tm = x_ref.shape[1]
    sub = tm // IN_SUBTILES
    rows = [slice(s * sub, (s + 1) * sub) for s in range(IN_SUBTILES)]
    projs = [{} for _ in range(IN_SUBTILES)]

    @pl.when(i == 0)
    def _():
        halo_ref[...] = jnp.zeros_like(halo_ref)

    def project(s):
        return _project_stages(rows[s], x_ref, g_ref, w_ref, projs[s])

    def finish(s):
        return _finish_stages(s, rows[s], i * tm + s * sub, projs[s], cos_ref, sin_ref, wpool_ref,
                              pscale_ref, halo_ref, pool_ref, q_refs, k_refs, v_refs, staging)

    _emit_interleaved(project(0), None)
    for s in range(IN_SUBTILES):
        _emit_interleaved(project(s + 1) if s + 1 < IN_SUBTILES else None, finish(s))


def _in_proj(x, g, w_in, cos_t, sin_t, wpool_bd, pscale):
    b, s, _ = x.shape
    tm = TOKEN_TILE
    row = lambda bi, i: (bi, i, 0)
    tab = lambda bi, i: (i, 0)
    stream = lambda bi, i: (bi, 0, 0, i, 0)
    stream_specs = [pl.BlockSpec((1, N_PAIRS, d, tm // d, LANES), stream) for d in DILATIONS] * 3
    stream_shapes = [_stream_shape(b, N_PAIRS, s, d, _BF16) for d in DILATIONS] * 3
    outs = pl.pallas_call(
        _in_proj_kernel,
        grid=(b, s // tm),
        in_specs=[
            pl.BlockSpec((1, tm, D_MODEL), row),
            _resident((1, D_MODEL)),
            _resident((D_MODEL, IN_WIDTH)),
            pl.BlockSpec((tm, LANES), tab),
            pl.BlockSpec((tm, LANES), tab),
            _resident((POOL_WIDTH, POOL_WIDTH)),
            _resident((1, POOL_WIDTH)),
        ],
        out_specs=[pl.BlockSpec((1, tm, POOL_WIDTH), row)] + stream_specs,
        out_shape=[jax.ShapeDtypeStruct((b, s, POOL_WIDTH), _BF16)] + stream_shapes,
        scratch_shapes=[
            pltpu.VMEM((MAX_POOL_WIN, POOL_WIDTH), _F32),
            pltpu.VMEM((N_STAGING, tm // IN_SUBTILES, LANES), _F32),
        ],
        compiler_params=pltpu.CompilerParams(
            dimension_semantics=("arbitrary", "arbitrary"), vmem_limit_bytes=VMEM_LIMIT),
        name="in_proj",
    )(x, g, w_in, cos_t, sin_t, wpool_bd, pscale)
    n_dil = len(DILATIONS)
    return outs[0], outs[1:1 + n_dil], outs[1 + n_dil:1 + 2 * n_dil], outs[1 + 2 * n_dil:]


def _blk_rows(blk):
    return slice(blk * BLOCK, (blk + 1) * BLOCK)


def _attn_kernel(q_ref, k_ref, v_ref, o_ref, lse_ref, kprev, vprev, bias_ref):
    c = pl.program_id(2)
    n_pairs, dilation, rows_per_res = q_ref.shape[1:4]
    per_res = rows_per_res // BLOCK

    @pl.when(c == 0)
    def _():
        kprev[...] = jnp.zeros_like(kprev)
        vprev[...] = jnp.zeros_like(vprev)

    qi = lax.broadcasted_iota(jnp.int32, (BLOCK, 2 * BLOCK), 0)
    kj = lax.broadcasted_iota(jnp.int32, (BLOCK, 2 * BLOCK), 1)
    dist = qi + BLOCK - kj
    band = (dist >= 0) & (dist <= BLOCK)
    band_first = band & (kj >= BLOCK * (1 - jnp.minimum(c, 1)))
    bias_ref[0] = jnp.where(band_first, 0.0, -jnp.inf)
    bias_ref[1] = jnp.where(band, 0.0, -jnp.inf)
    lane = lax.broadcasted_iota(jnp.int32, (BLOCK, LANES), 1)
    low_head = lane < HEAD_DIM
    ones_block = jnp.ones((2 * BLOCK, LANES), _BF16)
    zero_bf16 = jnp.zeros((BLOCK, LANES), _BF16)

    def band_of(ref, prev, p, r, i):
        head = prev[p, _blk_rows(r), :] if i == 0 else ref[0, p, r, _blk_rows(i - 1), :]
        return jnp.concatenate([head, ref[0, p, r, _blk_rows(i), :]], axis=0)

    def scores(p, r, i):
        q = q_ref[0, p, r, _blk_rows(i), :]
        q2 = jnp.concatenate([jnp.where(low_head, q, zero_bf16),
                              jnp.where(low_head, zero_bf16, q)], axis=0)
        return lax.dot_general(q2, band_of(k_ref, kprev, p, r, i), (((1,), (1,)), ((), ())),
                               preferred_element_type=_F32)

    def attend(p, r, i, sc):
        bias = bias_ref.at[0 if i == 0 else 1]
        maxes, probs = [], []
        for hh in range(HEADS_PER_PAIR):
            s_h = sc[hh * BLOCK:(hh + 1) * BLOCK, :] + bias[...]
            m = jnp.max(s_h, axis=-1, keepdims=True)
            maxes.append(m)
            probs.append(jnp.exp2(s_h - m).astype(_BF16))
        v_aug = jnp.concatenate([band_of(v_ref, vprev, p, r, i), ones_block], axis=1)
        pv = jnp.dot(jnp.concatenate(probs, axis=0), v_aug, preferred_element_type=_F32)
        num = jnp.where(low_head, pv[0:BLOCK, 0:LANES], pv[BLOCK:, 0:LANES])
        den = jnp.where(low_head, pv[0:BLOCK, LANES:], pv[BLOCK:, LANES:])
        o_ref[0, p, r, _blk_rows(i), :] = (num * (1.0 / den)).astype(o_ref.dtype)
        lse = jnp.where(low_head, maxes[0], maxes[1]) * LN2 + jnp.log(den)
        head_lanes = (lane % HEAD_DIM) == p
        if p == 0:
            lse_ref[0, 0, r, _blk_rows(i), :] = jnp.where(head_lanes, lse, 0.0)
        else:
            lse_ref[0, 0, r, _blk_rows(i), :] = jnp.where(head_lanes, lse,
                                                          lse_ref[0, 0, r, _blk_rows(i), :])

    work = [(p, r, i) for p in range(n_pairs) for r in range(dilation) for i in range(per_res)]
    sc_next = scores(*work[0])
    for n, item in enumerate(work):
        sc = sc_next
        if n + 1 < len(work):
            sc_next = scores(*work[n + 1])
        attend(*item, sc)
    for p in range(n_pairs):
        for r in range(dilation):
            kprev[p, _blk_rows(r), :] = k_ref[0, p, r, _blk_rows(per_res - 1), :]
            vprev[p, _blk_rows(r), :] = v_ref[0, p, r, _blk_rows(per_res - 1), :]


def _attention_config(q, k, v):
    b, _, dilation, length, _ = q.shape
    rows = ATTN_CHUNK // dilation
    pg = PAIRS_PER_GROUP
    stream = lambda bi, g, c: (bi, g, 0, c, 0)
    return pl.pallas_call(
        _attn_kernel,
        grid=(b, PAIR_GROUPS, length // rows),
        in_specs=[pl.BlockSpec((1, pg, dilation, rows, LANES), stream)] * 3,
        out_specs=(
            pl.BlockSpec((1, pg, dilation, rows, LANES), stream),
            pl.BlockSpec((1, 1, dilation, rows, LANES), stream),
        ),
        out_shape=(
            _stream_shape(b, N_PAIRS, dilation * length, dilation, _BF16),
            _stream_shape(b, PAIR_GROUPS, dilation * length, dilation, _F32),
        ),
        scratch_shapes=[
            pltpu.VMEM((pg, dilation * BLOCK, LANES), _BF16),
            pltpu.VMEM((pg, dilation * BLOCK, LANES), _BF16),
            pltpu.VMEM((2, BLOCK, 2 * BLOCK), _F32),
        ],
        compiler_params=pltpu.CompilerParams(
            dimension_semantics=("arbitrary", "arbitrary", "arbitrary"),
            vmem_limit_bytes=VMEM_LIMIT),
        name=f"attn_d{dilation}",
    )(q, k, v)


def _position_order(slab_ref, st_a, st_b):
    dilation, per_res, _ = slab_ref.shape
    if dilation == 1:
        return slab_ref[0].astype(_F32)
    if dilation == FAST_STRIDE:
        for r in range(dilation):
            st_a[pl.ds(r, per_res, stride=FAST_STRIDE), :] = slab_ref[r].astype(_F32)
        return st_a[...]
    quarter = dilation * per_res // FAST_STRIDE
    for r in range(dilation):
        a, b = r % FAST_STRIDE, r // FAST_STRIDE
        st_b[pl.ds(a * quarter + b, per_res, stride=FAST_STRIDE), :] = slab_ref[r].astype(_F32)
    for a in range(FAST_STRIDE):
        st_a[pl.ds(a, quarter, stride=FAST_STRIDE), :] = st_b[a * quarter:(a + 1) * quarter, :]
    return st_a[...]


def _merge_stages(x_ref, pool_ref, o_refs, l_refs, expand_ref, w_ref, staging, result):
    tm = x_ref.shape[1]
    lane = lax.broadcasted_iota(jnp.int32, (tm, LANES), 1)
    buffers = iter(range(10 ** 6))

    def ordered(slab_ref):
        j = next(buffers) % (N_STAGING // 2)
        return _position_order(slab_ref, staging.at[2 * j], staging.at[2 * j + 1])

    def head_lse(l_ref):
        return jnp.where((lane % HEAD_DIM) < PAIRS_PER_GROUP, ordered(l_ref.at[0, 0]),
                         pltpu.roll(ordered(l_ref.at[0, 1]), PAIRS_PER_GROUP, axis=1))

    l1, l2, l3 = (head_lse(l_ref) for l_ref in l_refs)
    mx = jnp.maximum(jnp.maximum(l1, l2), l3)
    e1, e2, e3 = jnp.exp(l1 - mx), jnp.exp(l2 - mx), jnp.exp(l3 - mx)
    inv = 1.0 / (e1 + e2 + e3)
    weights = [(e * inv).astype(_BF16) for e in (e1, e2, e3)]
    yield
    wexps = [jnp.dot(w, expand_ref[...], preferred_element_type=_F32) for w in weights]
    mix_pool = jnp.dot(pool_ref[0], w_ref[0:POOL_WIDTH, :], preferred_element_type=_F32)
    yield
    attn = None
    for wexp, o_ref in zip(wexps, o_refs):
        o = jnp.concatenate([ordered(o_ref.at[0, p]) for p in range(N_PAIRS)], axis=1)
        term = wexp * o
        attn = term if attn is None else attn + term
        yield
    result["mix"] = mix_pool + jnp.dot(attn.astype(_BF16), w_ref[POOL_WIDTH:, :],
                                       preferred_element_type=_F32)


def _gate_up_stages(h_early_ref, h_late_ref, wg_ref, wu_ref, act_ref):
    for c in range(D_FF // FF_CHUNK):
        h_ref = h_early_ref if c < H_EARLY_CHUNKS else h_late_ref
        cols = slice(c * FF_CHUNK, (c + 1) * FF_CHUNK)
        gate = jnp.dot(h_ref[...], wg_ref[:, cols], preferred_element_type=_F32)
        up = jnp.dot(h_ref[...], wu_ref[:, cols], preferred_element_type=_F32)
        act_ref[:, cols] = (gate * jax.nn.sigmoid(gate) * up).astype(act_ref.dtype)
        yield


def _post_kernel(x_ref, pool_ref, o1_ref, o2_ref, o3_ref, l1_ref, l2_ref, l3_ref, expand_ref,
                 wout_ref, gmix_ref, gpre_ref, wg_ref, wu_ref, wd_ref, gpost_ref, out_ref,
                 act_ref, x1_stage, h_stage, x1_prev, h_prev, staging):
    t = pl.program_id(0)

    @pl.when(t == 0)
    def _():
        x1_stage[...] = jnp.zeros_like(x1_stage)
        h_stage[...] = jnp.zeros_like(h_stage)

    def stage_tile():
        x1_prev[...] = x1_stage[...]
        yield
        h_prev[...] = h_stage[...]
        yield
        result = {}
        yield from _merge_stages(x_ref, pool_ref, (o1_ref, o2_ref, o3_ref), (l1_ref, l2_ref, l3_ref),
                                 expand_ref, wout_ref, staging, result)
        yield
        x1 = x_ref[0] + _rms_scale(result["mix"]) * gmix_ref[...]
        x1_stage[...] = x1
        h_stage[...] = (_rms_scale(x1) * gpre_ref[...]).astype(_BF16)

    _emit_interleaved(_gate_up_stages(h_stage, h_prev, wg_ref, wu_ref, act_ref), stage_tile())
    f = jnp.dot(act_ref[...], wd_ref[...], preferred_element_type=_F32)
    out_ref[0] = x1_prev[...] + _rms_scale(f) * gpost_ref[...]


def _post(x, pool, outs, lses, expand, w_out, gmix, gpre, wg, wu, wd, gpost):
    b, s, _ = x.shape
    tm = POST_TILE
    per_seq = s // tm
    n_tiles = b * per_seq
    merged = lambda t: jnp.minimum(t, n_tiles - 1)
    written = lambda t: jnp.maximum(t - 1, 0)
    row = lambda t: (merged(t) // per_seq, merged(t) % per_seq, 0)
    stream = lambda t: (merged(t) // per_seq, 0, 0, merged(t) % per_seq, 0)
    out_row = lambda t: (written(t) // per_seq, written(t) % per_seq, 0)
    return pl.pallas_call(
        _post_kernel,
        grid=(n_tiles + 1,),
        in_specs=[
            pl.BlockSpec((1, tm, D_MODEL), row),
            pl.BlockSpec((1, tm, POOL_WIDTH), row),
        ] + [pl.BlockSpec((1, N_PAIRS, d, tm // d, LANES), stream) for d in DILATIONS]
          + [pl.BlockSpec((1, PAIR_GROUPS, d, tm // d, LANES), stream) for d in DILATIONS] + [
            _resident((LANES, ATTN_WIDTH)),
            _resident((D_MODEL, D_MODEL)),
            _resident((1, D_MODEL)),
            _resident((1, D_MODEL)),
            _resident((D_MODEL, D_FF)),
            _resident((D_MODEL, D_FF)),
            _resident((D_FF, D_MODEL)),
            _resident((1, D_MODEL)),
        ],
        out_specs=pl.BlockSpec((1, tm, D_MODEL), out_row),
        out_shape=jax.ShapeDtypeStruct((b, s, D_MODEL), _F32),
        scratch_shapes=[
            pltpu.VMEM((tm, D_FF), _BF16),
            pltpu.VMEM((tm, D_MODEL), _F32),
            pltpu.VMEM((tm, D_MODEL), _BF16),
            pltpu.VMEM((tm, D_MODEL), _F32),
            pltpu.VMEM((tm, D_MODEL), _BF16),
            pltpu.VMEM((N_STAGING, tm, LANES), _F32),
        ],
        compiler_params=pltpu.CompilerParams(
            dimension_semantics=("arbitrary",), vmem_limit_bytes=VMEM_LIMIT),
        name="post",
    )(x, pool, *outs, *lses, expand, w_out, gmix, gpre, wg, wu, wd, gpost)


def _rope_tables(seq_len):
    freqs = ROPE_THETA ** (-jnp.arange(HALF_DIM, dtype=_F32) * (2.0 / HEAD_DIM))
    ang = jnp.arange(seq_len).astype(_F32)[:, None] * freqs[None, :]
    cos, sin = jnp.cos(ang), jnp.sin(ang)
    cos_t = jnp.concatenate([cos, cos, cos, cos], axis=1)
    sin_t = jnp.concatenate([-sin, sin, -sin, sin], axis=1)
    return cos_t, sin_t


def _block_diag(w_pool):
    n_g = w_pool.shape[0]
    out = jnp.zeros((POOL_WIDTH, POOL_WIDTH), w_pool.dtype)
    for gi in range(n_g):
        out = out.at[gi * POOL_GROUP:(gi + 1) * POOL_GROUP,
                     gi * POOL_GROUP:(gi + 1) * POOL_GROUP].set(w_pool[gi])
    return out


def _head_expand():
    e = np.zeros((LANES, ATTN_WIDTH), np.float32)
    for p in range(N_PAIRS):
        for hh in range(HEADS_PER_PAIR):
            h = HEADS_PER_PAIR * p + hh
            e[hh * HEAD_DIM + p, h * HEAD_DIM:(h + 1) * HEAD_DIM] = 1.0
    return jnp.asarray(e, dtype=_BF16)


def kernel(x, ln_pre_mix, w_in, w_pool, pool_scale, w_out, ln_post_mix, ln_pre_ffn,
           w_gate, w_up, w_down, ln_post_ffn):
    b, s, _ = x.shape
    depth = w_in.shape[0]
    cos_t, sin_t = _rope_tables(s)
    expand = _head_expand()
    for l in range(depth):
        pool, qs, ks, vs = _in_proj(
            x, ln_pre_mix[l][None, :], w_in[l].astype(_BF16), cos_t, sin_t,
            _block_diag(w_pool[l]).astype(_BF16), pool_scale[l][None, :])
        outs, lses = [], []
        for q, k, v in zip(qs, ks, vs):
            o, lse = _attention_config(q, k, v)
            outs.append(o)
            lses.append(lse)
        x = _post(x, pool, outs, lses, expand, w_out[l].astype(_BF16), ln_post_mix[l][None, :],
                  ln_pre_ffn[l][None, :], w_gate[l].astype(_BF16), w_up[l].astype(_BF16),
                  w_down[l].astype(_BF16), ln_post_ffn[l][None, :])
    return x
```

```python
import functools
import math

import jax
import jax.numpy as jnp
import numpy as np
from jax import lax
from jax.experimental import pallas as pl
from jax.experimental.pallas import tpu as pltpu

D_MODEL = 1024
POOL_WIDTH = 256
POOL_WINDOWS = (2, 4, 8, 16)
POOL_GROUP = 64
HEAD_DIM = 64
HALF_DIM = HEAD_DIM // 2
ATTN_WIDTH = 768
N_HEADS = 12
DILATED_CONFIGS = ((128, 1), (512, 4), (2048, 16))
DILATIONS = tuple(d for _, d in DILATED_CONFIGS)
BLOCK = 128
ROPE_THETA = 10000.0
IN_WIDTH = POOL_WIDTH + 3 * ATTN_WIDTH
D_FF = 2816
EPS = 1e-6

LANES = 128
MAX_POOL_WIN = max(POOL_WINDOWS)

N_PAIRS = ATTN_WIDTH // LANES
HEADS_PER_PAIR = LANES // HEAD_DIM

TOKEN_TILE = 1024
IN_SUBTILES = 4
POST_TILE = 512
ATTN_CHUNK = 2048
PAIR_GROUPS = 2
PAIRS_PER_GROUP = N_PAIRS // PAIR_GROUPS
FAST_STRIDE = 4
N_STAGING = 12
FF_CHUNK = 256
H_EARLY_CHUNKS = 3
VMEM_LIMIT = 56 * 1024 * 1024

LOG2E = math.log2(math.e)
LN2 = math.log(2.0)

_BF16 = jnp.bfloat16
_F32 = jnp.float32


def _rms_scale(x):
    return x * lax.rsqrt(jnp.mean(x * x, axis=-1, keepdims=True) + EPS)


def _resident(shape):
    nd = len(shape)
    return pl.BlockSpec(shape, lambda *_: (0,) * nd, pipeline_mode=pl.Buffered(1))


def _emit_interleaved(main, side):
    live = [g for g in (main, side) if g is not None]
    while live:
        for g in list(live):
            if next(g, StopIteration) is StopIteration:
                live.remove(g)


def _stream_shape(batch, lead, seq_len, dilation, dtype):
    return jax.ShapeDtypeStruct((batch, lead, dilation, seq_len // dilation, LANES), dtype)


def _rope_slab(xc, cos, sin_signed, first_half):
    ahead = pltpu.roll(xc, LANES - HALF_DIM, axis=1)
    behind = pltpu.roll(xc, HALF_DIM, axis=1)
    partner = jnp.where(first_half, ahead, behind)
    return xc * cos + partner * sin_signed


def _pool_mixer(u, first_pos, halo_ref, wpool_ref, pscale_ref):
    n = u.shape[0]
    ext = jnp.concatenate([halo_ref[...], u], axis=0)
    halo_ref[...] = u[n - MAX_POOL_WIN:, :]
    s2 = ext[1:, :] + ext[:-1, :]
    s4 = s2[2:, :] + s2[:-2, :]
    s8 = s4[4:, :] + s4[:-4, :]
    s16 = s8[8:, :] + s8[:-8, :]
    w2 = s2[MAX_POOL_WIN - 1:MAX_POOL_WIN - 1 + n, :]
    w4 = s4[MAX_POOL_WIN - 3:MAX_POOL_WIN - 3 + n, :]
    w8 = s8[MAX_POOL_WIN - 7:MAX_POOL_WIN - 7 + n, :]
    w16 = s16[1:1 + n, :]
    lane = lax.broadcasted_iota(jnp.int32, (n, POOL_WIDTH), 1)
    grp = lane // POOL_GROUP
    wsum = jnp.where(grp == 0, w2, jnp.where(grp == 1, w4, jnp.where(grp == 2, w8, w16)))
    win = jnp.where(grp == 0, 2, jnp.where(grp == 1, 4, jnp.where(grp == 2, 8, 16)))
    pos = first_pos + lax.broadcasted_iota(jnp.int32, (n, POOL_WIDTH), 0)
    cnt = jnp.minimum(pos + 1, win).astype(_F32)
    d = wsum / cnt - u
    y = jnp.dot(d.astype(_BF16), wpool_ref[...], preferred_element_type=_F32)
    return y * pscale_ref[...]


def _store_streams(t, st_a, st_b, out_refs, p, s, n):
    ref1, ref4, ref16 = out_refs
    ref1[0, p, 0, s * n:(s + 1) * n, :] = t.astype(ref1.dtype)
    st_a[...] = t
    quarter = n // FAST_STRIDE
    for a in range(FAST_STRIDE):
        rows_a = st_a[pl.ds(a, quarter, stride=FAST_STRIDE), :]
        ref4[0, p, a, s * quarter:(s + 1) * quarter, :] = rows_a.astype(ref4.dtype)
        st_b[a * quarter:(a + 1) * quarter, :] = rows_a
    per_res = n // DILATIONS[2]
    for r in range(DILATIONS[2]):
        a, b = r % FAST_STRIDE, r // FAST_STRIDE
        ref16[0, p, r, s * per_res:(s + 1) * per_res, :] = (
            st_b[pl.ds(a * quarter + b, per_res, stride=FAST_STRIDE), :].astype(ref16.dtype))


def _project_stages(rows, x_ref, g_ref, w_ref, proj):
    h = (_rms_scale(x_ref[0, rows, :]) * g_ref[...]).astype(_BF16)
    yield
    edges = (0, POOL_WIDTH, POOL_WIDTH + ATTN_WIDTH, POOL_WIDTH + 2 * ATTN_WIDTH, IN_WIDTH)
    for name, lo, hi in zip(("u", "q", "k", "v"), edges[:-1], edges[1:]):
        proj[name] = jnp.dot(h, w_ref[:, lo:hi], preferred_element_type=_F32)
        yield


def _finish_stages(s, rows, first_pos, proj, cos_ref, sin_ref, wpool_ref, pscale_ref, halo_ref,
                   pool_ref, q_refs, k_refs, v_refs, staging):
    n = rows.stop - rows.start
    pool_ref[0, rows, :] = _pool_mixer(proj["u"], first_pos, halo_ref, wpool_ref,
                                       pscale_ref).astype(pool_ref.dtype)
    yield
    cos = cos_ref[rows, :]
    sin_signed = sin_ref[rows, :]
    lane128 = lax.broadcasted_iota(jnp.int32, (n, LANES), 1)
    first_half = (lane128 % HEAD_DIM) < HALF_DIM
    q_scale = (HEAD_DIM ** -0.5) * LOG2E
    slab_count = 0
    for name, refs, scale in (("q", q_refs, q_scale), ("k", k_refs, None), ("v", v_refs, None)):
        for p in range(N_PAIRS):
            t = proj[name][:, p * LANES:(p + 1) * LANES]
            if name != "v":
                t = _rope_slab(t, cos, sin_signed, first_half)
            if scale is not None:
                t = t * scale
            j = slab_count % (N_STAGING // 2)
            _store_streams(t, staging.at[2 * j], staging.at[2 * j + 1], refs, p, s, n)
            slab_count += 1
            if p % 2 == 1:
                yield


def _in_proj_kernel(x_ref, g_ref, w_ref, cos_ref, sin_ref, wpool_ref, pscale_ref, pool_ref, *rest):
    n_dil = len(DILATIONS)
    q_refs, k_refs, v_refs = rest[0:n_dil], rest[n_dil:2 * n_dil], rest[2 * n_dil:3 * n_dil]
    halo_ref, staging = rest[3 * n_dil:]
    i = pl.program_id(1)
    tm = x_ref.shape[1]
    sub = tm // IN_SUBTILES
    rows = [slice(s * sub, (s + 1) * sub) for s in range(IN_SUBTILES)]
    projs = [{} for _ in range(IN_SUBTILES)]

    @pl.when(i == 0)
    def _():
        halo_ref[...] = jnp.zeros_like(halo_ref)

    def project(s):
        return _project_stages(rows[s], x_ref, g_ref, w_ref, projs[s])

    def finish(s):
        return _finish_stages(s, rows[s], i * tm + s * sub, projs[s], cos_ref, sin_ref, wpool_ref,
                              pscale_ref, halo_ref, pool_ref, q_refs, k_refs, v_refs, staging)

    _emit_interleaved(project(0), None)
    for s in range(IN_SUBTILES):
        _emit_interleaved(project(s + 1) if s + 1 < IN_SUBTILES else None, finish(s))


def _in_proj(x, g, w_in, cos_t, sin_t, wpool_bd, pscale):
    b, s, _ = x.shape
    tm = TOKEN_TILE
    row = lambda bi, i: (bi, i, 0)
    tab = lambda bi, i: (i, 0)
    stream = lambda bi, i: (bi, 0, 0, i, 0)
    stream_specs = [pl.BlockSpec((1, N_PAIRS, d, tm // d, LANES), stream) for d in DILATIONS] * 3
    stream_shapes = [_stream_shape(b, N_PAIRS, s, d, _BF16) for d in DILATIONS] * 3
    outs = pl.pallas_call(
        _in_proj_kernel,
        grid=(b, s // tm),
        in_specs=[
            pl.BlockSpec((1, tm, D_MODEL), row),
            _resident((1, D_MODEL)),
            _resident((D_MODEL, IN_WIDTH)),
            pl.BlockSpec((tm, LANES), tab),
            pl.BlockSpec((tm, LANES), tab),
            _resident((POOL_WIDTH, POOL_WIDTH)),
            _resident((1, POOL_WIDTH)),
        ],
        out_specs=[pl.BlockSpec((1, tm, POOL_WIDTH), row)] + stream_specs,
        out_shape=[jax.ShapeDtypeStruct((b, s, POOL_WIDTH), _BF16)] + stream_shapes,
        scratch_shapes=[
            pltpu.VMEM((MAX_POOL_WIN, POOL_WIDTH), _F32),
            pltpu.VMEM((N_STAGING, tm // IN_SUBTILES, LANES), _F32),
        ],
        compiler_params=pltpu.CompilerParams(
            dimension_semantics=("arbitrary", "arbitrary"), vmem_limit_bytes=VMEM_LIMIT),
        name="in_proj",
    )(x, g, w_in, cos_t, sin_t, wpool_bd, pscale)
    n_dil = len(DILATIONS)
    return outs[0], outs[1:1 + n_dil], outs[1 + n_dil:1 + 2 * n_dil], outs[1 + 2 * n_dil:]


def _blk_rows(blk):
    return slice(blk * BLOCK, (blk + 1) * BLOCK)


def _attn_kernel(q_ref, k_ref, v_ref, o_ref, lse_ref, kprev, vprev, bias_ref):
    c = pl.program_id(2)
    n_pairs, dilation, rows_per_res = q_ref.shape[1:4]
    per_res = rows_per_res // BLOCK

    @pl.when(c == 0)
    def _():
        kprev[...] = jnp.zeros_like(kprev)
        vprev[...] = jnp.zeros_like(vprev)

    qi = lax.broadcasted_iota(jnp.int32, (BLOCK, 2 * BLOCK), 0)
    kj = lax.broadcasted_iota(jnp.int32, (BLOCK, 2 * BLOCK), 1)
    dist = qi + BLOCK - kj
    band = (dist >= 0) & (dist <= BLOCK)
    band_first = band & (kj >= BLOCK * (1 - jnp.minimum(c, 1)))
    bias_ref[0] = jnp.where(band_first, 0.0, -jnp.inf)
    bias_ref[1] = jnp.where(band, 0.0, -jnp.inf)
    lane = lax.broadcasted_iota(jnp.int32, (BLOCK, LANES), 1)
    low_head = lane < HEAD_DIM
    ones_block = jnp.ones((2 * BLOCK, LANES), _BF16)
    zero_bf16 = jnp.zeros((BLOCK, LANES), _BF16)

    def band_of(ref, prev, p, r, i):
        head = prev[p, _blk_rows(r), :] if i == 0 else ref[0, p, r, _blk_rows(i - 1), :]
        return jnp.concatenate([head, ref[0, p, r, _blk_rows(i), :]], axis=0)

    def scores(p, r, i):
        q = q_ref[0, p, r, _blk_rows(i), :]
        q2 = jnp.concatenate([jnp.where(low_head, q, zero_bf16),
                              jnp.where(low_head, zero_bf16, q)], axis=0)
        return lax.dot_general(q2, band_of(k_ref, kprev, p, r, i), (((1,), (1,)), ((), ())),
                               preferred_element_type=_F32)

    def attend(p, r, i, sc):
        bias = bias_ref.at[0 if i == 0 else 1]
        maxes, probs = [], []
        for hh in range(HEADS_PER_PAIR):
            s_h = sc[hh * BLOCK:(hh + 1) * BLOCK, :] + bias[...]
            m = jnp.max(s_h, axis=-1, keepdims=True)
            maxes.append(m)
            probs.append(jnp.exp2(s_h - m).astype(_BF16))
        v_aug = jnp.concatenate([band_of(v_ref, vprev, p, r, i), ones_block], axis=1)
        pv = jnp.dot(jnp.concatenate(probs, axis=0), v_aug, preferred_element_type=_F32)
        num = jnp.where(low_head, pv[0:BLOCK, 0:LANES], pv[BLOCK:, 0:LANES])
        den = jnp.where(low_head, pv[0:BLOCK, LANES:], pv[BLOCK:, LANES:])
        o_ref[0, p, r, _blk_rows(i), :] = (num * (1.0 / den)).astype(o_ref.dtype)
        lse = jnp.where(low_head, maxes[0], maxes[1]) * LN2 + jnp.log(den)
        head_lanes = (lane % HEAD_DIM) == p
        if p == 0:
            lse_ref[0, 0, r, _blk_rows(i), :] = jnp.where(head_lanes, lse, 0.0)
        else:
            lse_ref[0, 0, r, _blk_rows(i), :] = jnp.where(head_lanes, lse,
                                                          lse_ref[0, 0, r, _blk_rows(i), :])

    work = [(p, r, i) for p in range(n_pairs) for r in range(dilation) for i in range(per_res)]
    sc_next = scores(*work[0])
    for n, item in enumerate(work):
        sc = sc_next
        if n + 1 < len(work):
            sc_next = scores(*work[n + 1])
        attend(*item, sc)
    for p in range(n_pairs):
        for r in range(dilation):
            kprev[p, _blk_rows(r), :] = k_ref[0, p, r, _blk_rows(per_res - 1), :]
            vprev[p, _blk_rows(r), :] = v_ref[0, p, r, _blk_rows(per_res - 1), :]


def _attention_config(q, k, v):
    b, _, dilation, length, _ = q.shape
    rows = ATTN_CHUNK // dilation
    pg = PAIRS_PER_GROUP
    stream = lambda bi, g, c: (bi, g, 0, c, 0)
    return pl.pallas_call(
        _attn_kernel,
        grid=(b, PAIR_GROUPS, length // rows),
        in_specs=[pl.BlockSpec((1, pg, dilation, rows, LANES), stream)] * 3,
        out_specs=(
            pl.BlockSpec((1, pg, dilation, rows, LANES), stream),
            pl.BlockSpec((1, 1, dilation, rows, LANES), stream),
        ),
        out_shape=(
            _stream_shape(b, N_PAIRS, dilation * length, dilation, _BF16),
            _stream_shape(b, PAIR_GROUPS, dilation * length, dilation, _F32),
        ),
        scratch_shapes=[
            pltpu.VMEM((pg, dilation * BLOCK, LANES), _BF16),
            pltpu.VMEM((pg, dilation * BLOCK, LANES), _BF16),
            pltpu.VMEM((2, BLOCK, 2 * BLOCK), _F32),
        ],
        compiler_params=pltpu.CompilerParams(
            dimension_semantics=("arbitrary", "arbitrary", "arbitrary"),
            vmem_limit_bytes=VMEM_LIMIT),
        name=f"attn_d{dilation}",
    )(q, k, v)


def _position_order(slab_ref, st_a, st_b):
    dilation, per_res, _ = slab_ref.shape
    if dilation == 1:
        return slab_ref[0].astype(_F32)
    if dilation == FAST_STRIDE:
        for r in range(dilation):
            st_a[pl.ds(r, per_res, stride=FAST_STRIDE), :] = slab_ref[r].astype(_F32)
        return st_a[...]
    quarter = dilation * per_res // FAST_STRIDE
    for r in range(dilation):
        a, b = r % FAST_STRIDE, r // FAST_STRIDE
        st_b[pl.ds(a * quarter + b, per_res, stride=FAST_STRIDE), :] = slab_ref[r].astype(_F32)
    for a in range(FAST_STRIDE):
        st_a[pl.ds(a, quarter, stride=FAST_STRIDE), :] = st_b[a * quarter:(a + 1) * quarter, :]
    return st_a[...]


def _merge_stages(x_ref, pool_ref, o_refs, l_refs, expand_ref, w_ref, staging, result):
    tm = x_ref.shape[1]
    lane = lax.broadcasted_iota(jnp.int32, (tm, LANES), 1)
    buffers = iter(range(10 ** 6))

    def ordered(slab_ref):
        j = next(buffers) % (N_STAGING // 2)
        return _position_order(slab_ref, staging.at[2 * j], staging.at[2 * j + 1])

    def head_lse(l_ref):
        return jnp.where((lane % HEAD_DIM) < PAIRS_PER_GROUP, ordered(l_ref.at[0, 0]),
                         pltpu.roll(ordered(l_ref.at[0, 1]), PAIRS_PER_GROUP, axis=1))

    l1, l2, l3 = (head_lse(l_ref) for l_ref in l_refs)
    mx = jnp.maximum(jnp.maximum(l1, l2), l3)
    e1, e2, e3 = jnp.exp(l1 - mx), jnp.exp(l2 - mx), jnp.exp(l3 - mx)
    inv = 1.0 / (e1 + e2 + e3)
    weights = [(e * inv).astype(_BF16) for e in (e1, e2)]
    yield
    wexps = [jnp.dot(w, expand_ref[...], preferred_element_type=_F32) for w in weights]
    wexps.append(1.0 - wexps[0] - wexps[1])
    mix_pool = jnp.dot(pool_ref[0], w_ref[0:POOL_WIDTH, :], preferred_element_type=_F32)
    yield
    attn = None
    for wexp, o_ref in zip(wexps, o_refs):
        o = jnp.concatenate([ordered(o_ref.at[0, p]) for p in range(N_PAIRS)], axis=1)
        term = wexp * o
        attn = term if attn is None else attn + term
        yield
    result["mix"] = mix_pool + jnp.dot(attn.astype(_BF16), w_ref[POOL_WIDTH:, :],
                                       preferred_element_type=_F32)


def _gate_up_stages(h_early_ref, h_late_ref, wg_ref, wu_ref, act_ref):
    for c in range(D_FF // FF_CHUNK):
        h_ref = h_early_ref if c < H_EARLY_CHUNKS else h_late_ref
        cols = slice(c * FF_CHUNK, (c + 1) * FF_CHUNK)
        gate = jnp.dot(h_ref[...], wg_ref[:, cols], preferred_element_type=_F32)
        up = jnp.dot(h_ref[...], wu_ref[:, cols], preferred_element_type=_F32)
        act_ref[:, cols] = (gate * jax.nn.sigmoid(gate) * up).astype(act_ref.dtype)
        yield


def _post_kernel(x_ref, pool_ref, o1_ref, o2_ref, o3_ref, l1_ref, l2_ref, l3_ref, expand_ref,
                 wout_ref, gmix_ref, gpre_ref, wg_ref, wu_ref, wd_ref, gpost_ref, out_ref,
                 act_ref, x1_stage, h_stage, x1_prev, h_prev, staging):
    t = pl.program_id(0)

    @pl.when(t == 0)
    def _():
        x1_stage[...] = jnp.zeros_like(x1_stage)
        h_stage[...] = jnp.zeros_like(h_stage)

    def stage_tile():
        x1_prev[...] = x1_stage[...]
        yield
        h_prev[...] = h_stage[...]
        yield
        result = {}
        yield from _merge_stages(x_ref, pool_ref, (o1_ref, o2_ref, o3_ref), (l1_ref, l2_ref, l3_ref),
                                 expand_ref, wout_ref, staging, result)
        yield
        x1 = x_ref[0] + _rms_scale(result["mix"]) * gmix_ref[...]
        x1_stage[...] = x1
        h_stage[...] = (_rms_scale(x1) * gpre_ref[...]).astype(_BF16)

    _emit_interleaved(_gate_up_stages(h_stage, h_prev, wg_ref, wu_ref, act_ref), stage_tile())
    f = jnp.dot(act_ref[...], wd_ref[...], preferred_element_type=_F32)
    out_ref[0] = x1_prev[...] + _rms_scale(f) * gpost_ref[...]


def _post(x, pool, outs, lses, expand, w_out, gmix, gpre, wg, wu, wd, gpost):
    b, s, _ = x.shape
    tm = POST_TILE
    per_seq = s // tm
    n_tiles = b * per_seq
    merged = lambda t: jnp.minimum(t, n_tiles - 1)
    written = lambda t: jnp.maximum(t - 1, 0)
    row = lambda t: (merged(t) // per_seq, merged(t) % per_seq, 0)
    stream = lambda t: (merged(t) // per_seq, 0, 0, merged(t) % per_seq, 0)
    out_row = lambda t: (written(t) // per_seq, written(t) % per_seq, 0)
    return pl.pallas_call(
        _post_kernel,
        grid=(n_tiles + 1,),
        in_specs=[
            pl.BlockSpec((1, tm, D_MODEL), row),
            pl.BlockSpec((1, tm, POOL_WIDTH), row),
        ] + [pl.BlockSpec((1, N_PAIRS, d, tm // d, LANES), stream) for d in DILATIONS]
          + [pl.BlockSpec((1, PAIR_GROUPS, d, tm // d, LANES), stream) for d in DILATIONS] + [
            _resident((LANES, ATTN_WIDTH)),
            _resident((D_MODEL, D_MODEL)),
            _resident((1, D_MODEL)),
            _resident((1, D_MODEL)),
            _resident((D_MODEL, D_FF)),
            _resident((D_MODEL, D_FF)),
            _resident((D_FF, D_MODEL)),
            _resident((1, D_MODEL)),
        ],
        out_specs=pl.BlockSpec((1, tm, D_MODEL), out_row),
        out_shape=jax.ShapeDtypeStruct((b, s, D_MODEL), _F32),
        scratch_shapes=[
            pltpu.VMEM((tm, D_FF), _BF16),
            pltpu.VMEM((tm, D_MODEL), _F32),
            pltpu.VMEM((tm, D_MODEL), _BF16),
            pltpu.VMEM((tm, D_MODEL), _F32),
            pltpu.VMEM((tm, D_MODEL), _BF16),
            pltpu.VMEM((N_STAGING, tm, LANES), _F32),
        ],
        compiler_params=pltpu.CompilerParams(
            dimension_semantics=("arbitrary",), vmem_limit_bytes=VMEM_LIMIT),
        name="post",
    )(x, pool, *outs, *lses, expand, w_out, gmix, gpre, wg, wu, wd, gpost)


def _rope_tables(seq_len):
    freqs = ROPE_THETA ** (-jnp.arange(HALF_DIM, dtype=_F32) * (2.0 / HEAD_DIM))
    ang = jnp.arange(seq_len).astype(_F32)[:, None] * freqs[None, :]
    cos, sin = jnp.cos(ang), jnp.sin(ang)
    cos_t = jnp.concatenate([cos, cos, cos, cos], axis=1)
    sin_t = jnp.concatenate([-sin, sin, -sin, sin], axis=1)
    return cos_t, sin_t


def _block_diag(w_pool):
    n_g = w_pool.shape[0]
    out = jnp.zeros((POOL_WIDTH, POOL_WIDTH), w_pool.dtype)
    for gi in range(n_g):
        out = out.at[gi * POOL_GROUP:(gi + 1) * POOL_GROUP,
                     gi * POOL_GROUP:(gi + 1) * POOL_GROUP].set(w_pool[gi])
    return out


def _head_expand():
    e = np.zeros((LANES, ATTN_WIDTH), np.float32)
    for p in range(N_PAIRS):
        for hh in range(HEADS_PER_PAIR):
            h = HEADS_PER_PAIR * p + hh
            e[hh * HEAD_DIM + p, h * HEAD_DIM:(h + 1) * HEAD_DIM] = 1.0
    return jnp.asarray(e, dtype=_BF16)


def kernel(x, ln_pre_mix, w_in, w_pool, pool_scale, w_out, ln_post_mix, ln_pre_ffn,
           w_gate, w_up, w_down, ln_post_ffn):
    b, s, _ = x.shape
    depth = w_in.shape[0]
    cos_t, sin_t = _rope_tables(s)
    expand = _head_expand()
    for l in range(depth):
        pool, qs, ks, vs = _in_proj(
            x, ln_pre_mix[l][None, :], w_in[l].astype(_BF16), cos_t, sin_t,
            _block_diag(w_pool[l]).astype(_BF16), pool_scale[l][None, :])
        outs, lses = [], []
        for q, k, v in zip(qs, ks, vs):
            o, lse = _attention_config(q, k, v)
            outs.append(o)
            lses.append(lse)
        x = _post(x, pool, outs, lses, expand, w_out[l].astype(_BF16), ln_post_mix[l][None, :],
                  ln_pre_ffn[l][None, :], w_gate[l].astype(_BF16), w_up[l].astype(_BF16),
                  w_down[l].astype(_BF16), ln_post_ffn[l][None, :])
    return x
```

```python
import functools
import math

import jax
import jax.numpy as jnp
import numpy as np
from jax import lax
from jax.experimental import pallas as pl
from jax.experimental.pallas import tpu as pltpu

D_MODEL = 1024
POOL_WIDTH = 256
POOL_WINDOWS = (2, 4, 8, 16)
POOL_GROUP = 64
HEAD_DIM = 64
HALF_DIM = HEAD_DIM // 2
ATTN_WIDTH = 768
N_HEADS = 12
DILATED_CONFIGS = ((128, 1), (512, 4), (2048, 16))
DILATIONS = tuple(d for _, d in DILATED_CONFIGS)
BLOCK = 128
ROPE_THETA = 10000.0
IN_WIDTH = POOL_WIDTH + 3 * ATTN_WIDTH
D_FF = 2816
EPS = 1e-6

LANES = 128
V7X_VMEM_BYTES = 64 * 1024 * 1024
MAX_POOL_WIN = max(POOL_WINDOWS)

N_PAIRS = ATTN_WIDTH // LANES
HEADS_PER_PAIR = LANES // HEAD_DIM

TOKEN_TILE = 1024
IN_SUBTILES = 4
POST_TILE = 512
ATTN_CHUNK = 2048
PAIR_GROUPS = 1
PAIRS_PER_GROUP = N_PAIRS // PAIR_GROUPS
FAST_STRIDE = 4
N_STAGING = 12
FF_CHUNK = 256
H_EARLY_CHUNKS = 3
VMEM_LIMIT = V7X_VMEM_BYTES * 7 // 8

LOG2E = math.log2(math.e)
LN2 = math.log(2.0)

_BF16 = jnp.bfloat16
_F32 = jnp.float32


def _rms_scale(x):
    return x * lax.rsqrt(jnp.mean(x * x, axis=-1, keepdims=True) + EPS)


def _resident(shape):
    nd = len(shape)
    return pl.BlockSpec(shape, lambda *_: (0,) * nd, pipeline_mode=pl.Buffered(1))


def _emit_interleaved(main, side):
    live = [g for g in (main, side) if g is not None]
    while live:
        for g in list(live):
            if next(g, StopIteration) is StopIteration:
                live.remove(g)


def _stream_shape(batch, lead, seq_len, dilation, dtype):
    return jax.ShapeDtypeStruct((batch, lead, dilation, seq_len // dilation, LANES), dtype)


def _rope_slab(xc, cos, sin_signed, first_half):
    ahead = pltpu.roll(xc, LANES - HALF_DIM, axis=1)
    behind = pltpu.roll(xc, HALF_DIM, axis=1)
    partner = jnp.where(first_half, ahead, behind)
    return xc * cos + partner * sin_signed


def _pool_mixer(u, first_pos, halo_ref, wpool_ref, pscale_ref):
    n = u.shape[0]
    ext = jnp.concatenate([halo_ref[...], u], axis=0)
    halo_ref[...] = u[n - MAX_POOL_WIN:, :]
    s2 = ext[1:, :] + ext[:-1, :]
    s4 = s2[2:, :] + s2[:-2, :]
    s8 = s4[4:, :] + s4[:-4, :]
    s16 = s8[8:, :] + s8[:-8, :]
    w2 = s2[MAX_POOL_WIN - 1:MAX_POOL_WIN - 1 + n, :]
    w4 = s4[MAX_POOL_WIN - 3:MAX_POOL_WIN - 3 + n, :]
    w8 = s8[MAX_POOL_WIN - 7:MAX_POOL_WIN - 7 + n, :]
    w16 = s16[1:1 + n, :]
    lane = lax.broadcasted_iota(jnp.int32, (n, POOL_WIDTH), 1)
    grp = lane // POOL_GROUP
    wsum = jnp.where(grp == 0, w2, jnp.where(grp == 1, w4, jnp.where(grp == 2, w8, w16)))
    win = jnp.where(grp == 0, 2, jnp.where(grp == 1, 4, jnp.where(grp == 2, 8, 16)))
    pos = first_pos + lax.broadcasted_iota(jnp.int32, (n, POOL_WIDTH), 0)
    cnt = jnp.minimum(pos + 1, win).astype(_F32)
    d = wsum / cnt - u
    y = jnp.dot(d.astype(_BF16), wpool_ref[...], preferred_element_type=_F32)
    return y * pscale_ref[...]


def _store_streams(t, st_a, st_b, out_refs, p, s, n):
    ref1, ref4, ref16 = out_refs
    ref1[0, p, 0, s * n:(s + 1) * n, :] = t.astype(ref1.dtype)
    st_a[...] = t
    quarter = n // FAST_STRIDE
    for a in range(FAST_STRIDE):
        rows_a = st_a[pl.ds(a, quarter, stride=FAST_STRIDE), :]
        ref4[0, p, a, s * quarter:(s + 1) * quarter, :] = rows_a.astype(ref4.dtype)
        st_b[a * quarter:(a + 1) * quarter, :] = rows_a
    per_res = n // DILATIONS[2]
    for r in range(DILATIONS[2]):
        a, b = r % FAST_STRIDE, r // FAST_STRIDE
        ref16[0, p, r, s * per_res:(s + 1) * per_res, :] = (
            st_b[pl.ds(a * quarter + b, per_res, stride=FAST_STRIDE), :].astype(ref16.dtype))


def _project_stages(rows, x_ref, g_ref, w_ref, proj):
    h = (_rms_scale(x_ref[0, rows, :]) * g_ref[...]).astype(_BF16)
    yield
    edges = (0, POOL_WIDTH, POOL_WIDTH + ATTN_WIDTH, POOL_WIDTH + 2 * ATTN_WIDTH, IN_WIDTH)
    for name, lo, hi in zip(("u", "q", "k", "v"), edges[:-1], edges[1:]):
        proj[name] = jnp.dot(h, w_ref[:, lo:hi], preferred_element_type=_F32)
        yield


def _finish_stages(s, rows, first_pos, proj, cos_ref, sin_ref, wpool_ref, pscale_ref, halo_ref,
                   pool_ref, q_refs, k_refs, v_refs, staging):
    n = rows.stop - rows.start
    pool_ref[0, rows, :] = _pool_mixer(proj["u"], first_pos, halo_ref, wpool_ref,
                                       pscale_ref).astype(pool_ref.dtype)
    yield
    cos = cos_ref[rows, :]
    sin_signed = sin_ref[rows, :]
    lane128 = lax.broadcasted_iota(jnp.int32, (n, LANES), 1)
    first_half = (lane128 % HEAD_DIM) < HALF_DIM
    q_scale = (HEAD_DIM ** -0.5) * LOG2E
    slab_count = 0
    for name, refs, scale in (("q", q_refs, q_scale), ("k", k_refs, None), ("v", v_refs, None)):
        for p in range(N_PAIRS):
            t = proj[name][:, p * LANES:(p + 1) * LANES]
            if name != "v":
                t = _rope_slab(t, cos, sin_signed, first_half)
            if scale is not None:
                t = t * scale
            j = slab_count % (N_STAGING // 2)
            _store_streams(t, staging.at[2 * j], staging.at[2 * j + 1], refs, p, s, n)
            slab_count += 1
            if p % 2 == 1:
                yield


def _in_proj_kernel(x_ref, g_ref, w_ref, cos_ref, sin_ref, wpool_ref, pscale_ref, pool_ref, *rest):
    n_dil = len(DILATIONS)
    q_refs, k_refs, v_refs = rest[0:n_dil], rest[n_dil:2 * n_dil], rest[2 * n_dil:3 * n_dil]
    halo_ref, staging = rest[3 * n_dil:]
    i = pl.program_id(1)
    tm = x_ref.shape[1]
    sub = tm // IN_SUBTILES
    rows = [slice(s * sub, (s + 1) * sub) for s in range(IN_SUBTILES)]
    projs = [{} for _ in range(IN_SUBTILES)]

    @pl.when(i == 0)
    def _():
        halo_ref[...] = jnp.zeros_like(halo_ref)

    def project(s):
        return _project_stages(rows[s], x_ref, g_ref, w_ref, projs[s])

    def finish(s):
        return _finish_stages(s, rows[s], i * tm + s * sub, projs[s], cos_ref, sin_ref, wpool_ref,
                              pscale_ref, halo_ref, pool_ref, q_refs, k_refs, v_refs, staging)

    _emit_interleaved(project(0), None)
    for s in range(IN_SUBTILES):
        _emit_interleaved(project(s + 1) if s + 1 < IN_SUBTILES else None, finish(s))


def _in_proj(x, g, w_in, cos_t, sin_t, wpool_bd, pscale):
    b, s, _ = x.shape
    tm = TOKEN_TILE
    row = lambda bi, i: (bi, i, 0)
    tab = lambda bi, i: (i, 0)
    stream = lambda bi, i: (bi, 0, 0, i, 0)
    stream_specs = [pl.BlockSpec((1, N_PAIRS, d, tm // d, LANES), stream) for d in DILATIONS] * 3
    stream_shapes = [_stream_shape(b, N_PAIRS, s, d, _BF16) for d in DILATIONS] * 3
    outs = pl.pallas_call(
        _in_proj_kernel,
        grid=(b, s // tm),
        in_specs=[
            pl.BlockSpec((1, tm, D_MODEL), row),
            _resident((1, D_MODEL)),
            _resident((D_MODEL, IN_WIDTH)),
            pl.BlockSpec((tm, LANES), tab),
            pl.BlockSpec((tm, LANES), tab),
            _resident((POOL_WIDTH, POOL_WIDTH)),
            _resident((1, POOL_WIDTH)),
        ],
        out_specs=[pl.BlockSpec((1, tm, POOL_WIDTH), row)] + stream_specs,
        out_shape=[jax.ShapeDtypeStruct((b, s, POOL_WIDTH), _BF16)] + stream_shapes,
        scratch_shapes=[
            pltpu.VMEM((MAX_POOL_WIN, POOL_WIDTH), _F32),
            pltpu.VMEM((N_STAGING, tm // IN_SUBTILES, LANES), _F32),
        ],
        compiler_params=pltpu.CompilerParams(
            dimension_semantics=("arbitrary", "arbitrary"), vmem_limit_bytes=VMEM_LIMIT),
        name="in_proj",
    )(x, g, w_in, cos_t, sin_t, wpool_bd, pscale)
    n_dil = len(DILATIONS)
    return outs[0], outs[1:1 + n_dil], outs[1 + n_dil:1 + 2 * n_dil], outs[1 + 2 * n_dil:]


def _blk_rows(blk):
    return slice(blk * BLOCK, (blk + 1) * BLOCK)


def _attn_kernel(q_ref, k_ref, v_ref, o_ref, lse_ref, kprev, vprev, bias_ref):
    c = pl.program_id(2)
    n_pairs, dilation, rows_per_res = q_ref.shape[1:4]
    per_res = rows_per_res // BLOCK

    @pl.when(c == 0)
    def _():
        kprev[...] = jnp.zeros_like(kprev)
        vprev[...] = jnp.zeros_like(vprev)

    qi = lax.broadcasted_iota(jnp.int32, (BLOCK, 2 * BLOCK), 0)
    kj = lax.broadcasted_iota(jnp.int32, (BLOCK, 2 * BLOCK), 1)
    dist = qi + BLOCK - kj
    band = (dist >= 0) & (dist <= BLOCK)
    band_first = band & (kj >= BLOCK * (1 - jnp.minimum(c, 1)))
    bias_ref[0] = jnp.where(band_first, 0.0, -jnp.inf)
    bias_ref[1] = jnp.where(band, 0.0, -jnp.inf)
    lane = lax.broadcasted_iota(jnp.int32, (BLOCK, LANES), 1)
    low_head = lane < HEAD_DIM
    ones_block = jnp.ones((2 * BLOCK, LANES), _BF16)
    zero_bf16 = jnp.zeros((BLOCK, LANES), _BF16)

    def band_of(ref, prev, p, r, i):
        head = prev[p, _blk_rows(r), :] if i == 0 else ref[0, p, r, _blk_rows(i - 1), :]
        return jnp.concatenate([head, ref[0, p, r, _blk_rows(i), :]], axis=0)

    def scores(p, r, i):
        q = q_ref[0, p, r, _blk_rows(i), :]
        q2 = jnp.concatenate([jnp.where(low_head, q, zero_bf16),
                              jnp.where(low_head, zero_bf16, q)], axis=0)
        return lax.dot_general(q2, band_of(k_ref, kprev, p, r, i), (((1,), (1,)), ((), ())),
                               preferred_element_type=_F32)

    def attend(p, r, i, sc):
        bias = bias_ref.at[0 if i == 0 else 1]
        maxes, probs = [], []
        for hh in range(HEADS_PER_PAIR):
            s_h = sc[hh * BLOCK:(hh + 1) * BLOCK, :] + bias[...]
            m = jnp.max(s_h, axis=-1, keepdims=True)
            maxes.append(m)
            probs.append(jnp.exp2(s_h - m).astype(_BF16))
        v_aug = jnp.concatenate([band_of(v_ref, vprev, p, r, i), ones_block], axis=1)
        pv = jnp.dot(jnp.concatenate(probs, axis=0), v_aug, preferred_element_type=_F32)
        num = jnp.where(low_head, pv[0:BLOCK, 0:LANES], pv[BLOCK:, 0:LANES])
        den = jnp.where(low_head, pv[0:BLOCK, LANES:], pv[BLOCK:, LANES:])
        o_ref[0, p, r, _blk_rows(i), :] = (num * (1.0 / den)).astype(o_ref.dtype)
        lse = jnp.where(low_head, maxes[0], maxes[1]) * LN2 + jnp.log(den)
        head_lanes = (lane % HEAD_DIM) == p
        if p == 0:
            lse_ref[0, 0, r, _blk_rows(i), :] = jnp.where(head_lanes, lse, 0.0)
        else:
            lse_ref[0, 0, r, _blk_rows(i), :] = jnp.where(head_lanes, lse,
                                                          lse_ref[0, 0, r, _blk_rows(i), :])

    work = [(p, r, i) for p in range(n_pairs) for r in range(dilation) for i in range(per_res)]
    sc_next = scores(*work[0])
    for n, item in enumerate(work):
        sc = sc_next
        if n + 1 < len(work):
            sc_next = scores(*work[n + 1])
        attend(*item, sc)
    for p in range(n_pairs):
        for r in range(dilation):
            kprev[p, _blk_rows(r), :] = k_ref[0, p, r, _blk_rows(per_res - 1), :]
            vprev[p, _blk_rows(r), :] = v_ref[0, p, r, _blk_rows(per_res - 1), :]


def _attention_config(q, k, v):
    b, _, dilation, length, _ = q.shape
    rows = ATTN_CHUNK // dilation
    pg = PAIRS_PER_GROUP
    stream = lambda bi, g, c: (bi, g, 0, c, 0)
    return pl.pallas_call(
        _attn_kernel,
        grid=(b, PAIR_GROUPS, length // rows),
        in_specs=[pl.BlockSpec((1, pg, dilation, rows, LANES), stream)] * 3,
        out_specs=(
            pl.BlockSpec((1, pg, dilation, rows, LANES), stream),
            pl.BlockSpec((1, 1, dilation, rows, LANES), stream),
        ),
        out_shape=(
            _stream_shape(b, N_PAIRS, dilation * length, dilation, _BF16),
            _stream_shape(b, PAIR_GROUPS, dilation * length, dilation, _F32),
        ),
        scratch_shapes=[
            pltpu.VMEM((pg, dilation * BLOCK, LANES), _BF16),
            pltpu.VMEM((pg, dilation * BLOCK, LANES), _BF16),
            pltpu.VMEM((2, BLOCK, 2 * BLOCK), _F32),
        ],
        compiler_params=pltpu.CompilerParams(
            dimension_semantics=("arbitrary", "arbitrary", "arbitrary"),
            vmem_limit_bytes=VMEM_LIMIT),
        name=f"attn_d{dilation}",
    )(q, k, v)


def _position_order(slab_ref, st_a, st_b):
    dilation, per_res, _ = slab_ref.shape
    if dilation == 1:
        return slab_ref[0].astype(_F32)
    if dilation == FAST_STRIDE:
        for r in range(dilation):
            st_a[pl.ds(r, per_res, stride=FAST_STRIDE), :] = slab_ref[r].astype(_F32)
        return st_a[...]
    quarter = dilation * per_res // FAST_STRIDE
    for r in range(dilation):
        a, b = r % FAST_STRIDE, r // FAST_STRIDE
        st_b[pl.ds(a * quarter + b, per_res, stride=FAST_STRIDE), :] = slab_ref[r].astype(_F32)
    for a in range(FAST_STRIDE):
        st_a[pl.ds(a, quarter, stride=FAST_STRIDE), :] = st_b[a * quarter:(a + 1) * quarter, :]
    return st_a[...]


def _merge_stages(x_ref, pool_ref, o_refs, l_refs, expand_ref, w_ref, staging, result):
    tm = x_ref.shape[1]
    lane = lax.broadcasted_iota(jnp.int32, (tm, LANES), 1)
    buffers = iter(range(10 ** 6))

    def ordered(slab_ref):
        j = next(buffers) % (N_STAGING // 2)
        return _position_order(slab_ref, staging.at[2 * j], staging.at[2 * j + 1])

    def head_lse(l_ref):
        out = ordered(l_ref.at[0, 0])
        for g in range(1, PAIR_GROUPS):
            shifted = pltpu.roll(ordered(l_ref.at[0, g]), g * PAIRS_PER_GROUP, axis=1)
            out = jnp.where((lane % HEAD_DIM) < g * PAIRS_PER_GROUP, out, shifted)
        return out

    l1, l2, l3 = (head_lse(l_ref) for l_ref in l_refs)
    mx = jnp.maximum(jnp.maximum(l1, l2), l3)
    e1, e2, e3 = jnp.exp(l1 - mx), jnp.exp(l2 - mx), jnp.exp(l3 - mx)
    inv = 1.0 / (e1 + e2 + e3)
    weights = [(e * inv).astype(_BF16) for e in (e1, e2)]
    yield
    wexps = [jnp.dot(w, expand_ref[...], preferred_element_type=_F32) for w in weights]
    wexps.append(1.0 - wexps[0] - wexps[1])
    mix_pool = jnp.dot(pool_ref[0], w_ref[0:POOL_WIDTH, :], preferred_element_type=_F32)
    yield
    attn = None
    for wexp, o_ref in zip(wexps, o_refs):
        o = jnp.concatenate([ordered(o_ref.at[0, p]) for p in range(N_PAIRS)], axis=1)
        term = wexp * o
        attn = term if attn is None else attn + term
        yield
    result["mix"] = mix_pool + jnp.dot(attn.astype(_BF16), w_ref[POOL_WIDTH:, :],
                                       preferred_element_type=_F32)


def _gate_up_stages(h_early_ref, h_late_ref, wg_ref, wu_ref, act_ref):
    for c in range(D_FF // FF_CHUNK):
        h_ref = h_early_ref if c < H_EARLY_CHUNKS else h_late_ref
        cols = slice(c * FF_CHUNK, (c + 1) * FF_CHUNK)
        gate = jnp.dot(h_ref[...], wg_ref[:, cols], preferred_element_type=_F32)
        up = jnp.dot(h_ref[...], wu_ref[:, cols], preferred_element_type=_F32)
        act_ref[:, cols] = (gate * jax.nn.sigmoid(gate) * up).astype(act_ref.dtype)
        yield


def _post_kernel(x_ref, pool_ref, o1_ref, o2_ref, o3_ref, l1_ref, l2_ref, l3_ref, expand_ref,
                 wout_ref, gmix_ref, gpre_ref, wg_ref, wu_ref, wd_ref, gpost_ref, out_ref,
                 act_ref, x1_stage, h_stage, x1_prev, h_prev, staging):
    t = pl.program_id(0)

    @pl.when(t == 0)
    def _():
        x1_stage[...] = jnp.zeros_like(x1_stage)
        h_stage[...] = jnp.zeros_like(h_stage)

    def stage_tile():
        x1_prev[...] = x1_stage[...]
        yield
        h_prev[...] = h_stage[...]
        yield
        result = {}
        yield from _merge_stages(x_ref, pool_ref, (o1_ref, o2_ref, o3_ref), (l1_ref, l2_ref, l3_ref),
                                 expand_ref, wout_ref, staging, result)
        yield
        x1 = x_ref[0] + _rms_scale(result["mix"]) * gmix_ref[...]
        x1_stage[...] = x1
        h_stage[...] = (_rms_scale(x1) * gpre_ref[...]).astype(_BF16)

    _emit_interleaved(_gate_up_stages(h_stage, h_prev, wg_ref, wu_ref, act_ref), stage_tile())
    f = jnp.dot(act_ref[...], wd_ref[...], preferred_element_type=_F32)
    out_ref[0] = x1_prev[...] + _rms_scale(f) * gpost_ref[...]


def _post(x, pool, outs, lses, expand, w_out, gmix, gpre, wg, wu, wd, gpost):
    b, s, _ = x.shape
    tm = POST_TILE
    per_seq = s // tm
    n_tiles = b * per_seq
    merged = lambda t: jnp.minimum(t, n_tiles - 1)
    written = lambda t: jnp.maximum(t - 1, 0)
    row = lambda t: (merged(t) // per_seq, merged(t) % per_seq, 0)
    stream = lambda t: (merged(t) // per_seq, 0, 0, merged(t) % per_seq, 0)
    out_row = lambda t: (written(t) // per_seq, written(t) % per_seq, 0)
    return pl.pallas_call(
        _post_kernel,
        grid=(n_tiles + 1,),
        in_specs=[
            pl.BlockSpec((1, tm, D_MODEL), row),
            pl.BlockSpec((1, tm, POOL_WIDTH), row),
        ] + [pl.BlockSpec((1, N_PAIRS, d, tm // d, LANES), stream) for d in DILATIONS]
          + [pl.BlockSpec((1, PAIR_GROUPS, d, tm // d, LANES), stream) for d in DILATIONS] + [
            _resident((LANES, ATTN_WIDTH)),
            _resident((D_MODEL, D_MODEL)),
            _resident((1, D_MODEL)),
            _resident((1, D_MODEL)),
            _resident((D_MODEL, D_FF)),
            _resident((D_MODEL, D_FF)),
            _resident((D_FF, D_MODEL)),
            _resident((1, D_MODEL)),
        ],
        out_specs=pl.BlockSpec((1, tm, D_MODEL), out_row),
        out_shape=jax.ShapeDtypeStruct((b, s, D_MODEL), _F32),
        scratch_shapes=[
            pltpu.VMEM((tm, D_FF), _BF16),
            pltpu.VMEM((tm, D_MODEL), _F32),
            pltpu.VMEM((tm, D_MODEL), _BF16),
            pltpu.VMEM((tm, D_MODEL), _F32),
            pltpu.VMEM((tm, D_MODEL), _BF16),
            pltpu.VMEM((N_STAGING, tm, LANES), _F32),
        ],
        compiler_params=pltpu.CompilerParams(
            dimension_semantics=("arbitrary",), vmem_limit_bytes=VMEM_LIMIT),
        name="post",
    )(x, pool, *outs, *lses, expand, w_out, gmix, gpre, wg, wu, wd, gpost)


def _rope_tables(seq_len):
    freqs = ROPE_THETA ** (-jnp.arange(HALF_DIM, dtype=_F32) * (2.0 / HEAD_DIM))
    ang = jnp.arange(seq_len).astype(_F32)[:, None] * freqs[None, :]
    cos, sin = jnp.cos(ang), jnp.sin(ang)
    cos_t = jnp.concatenate([cos, cos, cos, cos], axis=1)
    sin_t = jnp.concatenate([-sin, sin, -sin, sin], axis=1)
    return cos_t, sin_t


def _block_diag(w_pool):
    n_g = w_pool.shape[0]
    out = jnp.zeros((POOL_WIDTH, POOL_WIDTH), w_pool.dtype)
    for gi in range(n_g):
        out = out.at[gi * POOL_GROUP:(gi + 1) * POOL_GROUP,
                     gi * POOL_GROUP:(gi + 1) * POOL_GROUP].set(w_pool[gi])
    return out


def _head_expand():
    e = np.zeros((LANES, ATTN_WIDTH), np.float32)
    for p in range(N_PAIRS):
        for hh in range(HEADS_PER_PAIR):
            h = HEADS_PER_PAIR * p + hh
            e[hh * HEAD_DIM + p, h * HEAD_DIM:(h + 1) * HEAD_DIM] = 1.0
    return jnp.asarray(e, dtype=_BF16)


def kernel(x, ln_pre_mix, w_in, w_pool, pool_scale, w_out, ln_post_mix, ln_pre_ffn,
           w_gate, w_up, w_down, ln_post_ffn):
    b, s, _ = x.shape
    depth = w_in.shape[0]
    cos_t, sin_t = _rope_tables(s)
    expand = _head_expand()
    for l in range(depth):
        pool, qs, ks, vs = _in_proj(
            x, ln_pre_mix[l][None, :], w_in[l].astype(_BF16), cos_t, sin_t,
            _block_diag(w_pool[l]).astype(_BF16), pool_scale[l][None, :])
        outs, lses = [], []
        for q, k, v in zip(qs, ks, vs):
            o, lse = _attention_config(q, k, v)
            outs.append(o)
            lses.append(lse)
        x = _post(x, pool, outs, lses, expand, w_out[l].astype(_BF16), ln_post_mix[l][None, :],
                  ln_pre_ffn[l][None, :], w_gate[l].astype(_BF16), w_up[l].astype(_BF16),
                  w_down[l].astype(_BF16), ln_post_ffn[l][None, :])
    return x
```

```python
import itertools
import math

import jax
import jax.numpy as jnp
import numpy as np
from jax import lax
from jax.experimental import pallas as pl
from jax.experimental.pallas import tpu as pltpu

D_MODEL = 1024
POOL_WIDTH = 256
POOL_WINDOWS = (2, 4, 8, 16)
POOL_GROUP = 64
HEAD_DIM = 64
HALF_DIM = HEAD_DIM // 2
ATTN_WIDTH = 768
N_HEADS = 12
DILATED_CONFIGS = ((128, 1), (512, 4), (2048, 16))
DILATIONS = tuple(d for _, d in DILATED_CONFIGS)
BLOCK = 128
ROPE_THETA = 10000.0
IN_WIDTH = POOL_WIDTH + 3 * ATTN_WIDTH
D_FF = 2816
EPS = 1e-6

LANES = 128
V7X_VMEM_BYTES = 64 * 1024 * 1024
MAX_POOL_WIN = max(POOL_WINDOWS)

N_PAIRS = ATTN_WIDTH // LANES
HEADS_PER_PAIR = LANES // HEAD_DIM

TOKEN_TILE = 1024
IN_SUBTILES = 4
POST_TILE = 512
ATTN_CHUNK = 2048
PAIR_GROUPS = 1
PAIRS_PER_GROUP = N_PAIRS // PAIR_GROUPS
FAST_STRIDE = 4
N_STAGING = 12
FF_CHUNK = 256
H_EARLY_CHUNKS = 3
VMEM_LIMIT = V7X_VMEM_BYTES * 7 // 8

LOG2E = math.log2(math.e)
LN2 = math.log(2.0)

_BF16 = jnp.bfloat16
_F32 = jnp.float32


def _rms_scale(x):
    return x * lax.rsqrt(jnp.mean(x * x, axis=-1, keepdims=True) + EPS)


def _resident(shape):
    nd = len(shape)
    return pl.BlockSpec(shape, lambda *_: (0,) * nd, pipeline_mode=pl.Buffered(1))


def _emit_interleaved(main, side):
    live = [g for g in (main, side) if g is not None]
    while live:
        for g in list(live):
            if next(g, StopIteration) is StopIteration:
                live.remove(g)


def _stream_shape(batch, lead, seq_len, dilation, dtype):
    return jax.ShapeDtypeStruct((batch, lead, dilation, seq_len // dilation, LANES), dtype)


def _rope_slab(xc, cos, sin_signed, first_half):
    ahead = pltpu.roll(xc, LANES - HALF_DIM, axis=1)
    behind = pltpu.roll(xc, HALF_DIM, axis=1)
    partner = jnp.where(first_half, ahead, behind)
    return xc * cos + partner * sin_signed


def _pool_mixer(u, first_pos, halo_ref, wpool_ref, pscale_ref):
    n = u.shape[0]
    ext = jnp.concatenate([halo_ref[...], u], axis=0)
    halo_ref[...] = u[n - MAX_POOL_WIN:, :]
    s2 = ext[1:, :] + ext[:-1, :]
    s4 = s2[2:, :] + s2[:-2, :]
    s8 = s4[4:, :] + s4[:-4, :]
    s16 = s8[8:, :] + s8[:-8, :]
    w2 = s2[MAX_POOL_WIN - 1:MAX_POOL_WIN - 1 + n, :]
    w4 = s4[MAX_POOL_WIN - 3:MAX_POOL_WIN - 3 + n, :]
    w8 = s8[MAX_POOL_WIN - 7:MAX_POOL_WIN - 7 + n, :]
    w16 = s16[1:1 + n, :]
    lane = lax.broadcasted_iota(jnp.int32, (n, POOL_WIDTH), 1)
    grp = lane // POOL_GROUP
    wsum = jnp.where(grp == 0, w2, jnp.where(grp == 1, w4, jnp.where(grp == 2, w8, w16)))
    win = jnp.where(grp == 0, POOL_WINDOWS[0], jnp.where(grp == 1, POOL_WINDOWS[1],
                    jnp.where(grp == 2, POOL_WINDOWS[2], POOL_WINDOWS[3])))
    pos = first_pos + lax.broadcasted_iota(jnp.int32, (n, POOL_WIDTH), 0)
    cnt = jnp.minimum(pos + 1, win).astype(_F32)
    d = wsum / cnt - u
    y = jnp.dot(d.astype(_BF16), wpool_ref[...], preferred_element_type=_F32)
    return y * pscale_ref[...]


def _store_streams(t, st_a, st_b, out_refs, p, s, n):
    ref1, ref4, ref16 = out_refs
    ref1[0, p, 0, s * n:(s + 1) * n, :] = t.astype(ref1.dtype)
    st_a[...] = t
    quarter = n // FAST_STRIDE
    for a in range(FAST_STRIDE):
        rows_a = st_a[pl.ds(a, quarter, stride=FAST_STRIDE), :]
        ref4[0, p, a, s * quarter:(s + 1) * quarter, :] = rows_a.astype(ref4.dtype)
        st_b[a * quarter:(a + 1) * quarter, :] = rows_a
    per_res = n // DILATIONS[2]
    for r in range(DILATIONS[2]):
        a, b = r % FAST_STRIDE, r // FAST_STRIDE
        ref16[0, p, r, s * per_res:(s + 1) * per_res, :] = (
            st_b[pl.ds(a * quarter + b, per_res, stride=FAST_STRIDE), :].astype(ref16.dtype))


def _project_stages(rows, x_ref, g_ref, w_ref, proj):
    h = (_rms_scale(x_ref[0, rows, :]) * g_ref[...]).astype(_BF16)
    yield
    edges = (0, POOL_WIDTH, POOL_WIDTH + ATTN_WIDTH, POOL_WIDTH + 2 * ATTN_WIDTH, IN_WIDTH)
    for name, lo, hi in zip(("u", "q", "k", "v"), edges[:-1], edges[1:]):
        proj[name] = jnp.dot(h, w_ref[:, lo:hi], preferred_element_type=_F32)
        yield


def _finish_stages(s, rows, first_pos, proj, cos_ref, sin_ref, wpool_ref, pscale_ref, halo_ref,
                   pool_ref, q_refs, k_refs, v_refs, staging):
    n = rows.stop - rows.start
    pool_ref[0, rows, :] = _pool_mixer(proj["u"], first_pos, halo_ref, wpool_ref,
                                       pscale_ref).astype(pool_ref.dtype)
    yield
    cos = cos_ref[rows, :]
    sin_signed = sin_ref[rows, :]
    lane128 = lax.broadcasted_iota(jnp.int32, (n, LANES), 1)
    first_half = (lane128 % HEAD_DIM) < HALF_DIM
    q_scale = (HEAD_DIM ** -0.5) * LOG2E
    slab_count = 0
    for name, refs, scale in (("q", q_refs, q_scale), ("k", k_refs, None), ("v", v_refs, None)):
        for p in range(N_PAIRS):
            t = proj[name][:, p * LANES:(p + 1) * LANES]
            if name != "v":
                t = _rope_slab(t, cos, sin_signed, first_half)
            if scale is not None:
                t = t * scale
            j = slab_count % (N_STAGING // 2)
            _store_streams(t, staging.at[2 * j], staging.at[2 * j + 1], refs, p, s, n)
            slab_count += 1
            if p % 2 == 1:
                yield


def _in_proj_kernel(x_ref, g_ref, w_ref, cos_ref, sin_ref, wpool_ref, pscale_ref, pool_ref, *rest):
    n_dil = len(DILATIONS)
    q_refs, k_refs, v_refs = rest[0:n_dil], rest[n_dil:2 * n_dil], rest[2 * n_dil:3 * n_dil]
    halo_ref, staging = rest[3 * n_dil:]
    i = pl.program_id(1)
    tm = x_ref.shape[1]
    sub = tm // IN_SUBTILES
    rows = [slice(s * sub, (s + 1) * sub) for s in range(IN_SUBTILES)]
    projs = [{} for _ in range(IN_SUBTILES)]

    @pl.when(i == 0)
    def _():
        halo_ref[...] = jnp.zeros_like(halo_ref)

    def project(s):
        return _project_stages(rows[s], x_ref, g_ref, w_ref, projs[s])

    def finish(s):
        return _finish_stages(s, rows[s], i * tm + s * sub, projs[s], cos_ref, sin_ref, wpool_ref,
                              pscale_ref, halo_ref, pool_ref, q_refs, k_refs, v_refs, staging)

    _emit_interleaved(project(0), None)
    for s in range(IN_SUBTILES):
        _emit_interleaved(project(s + 1) if s + 1 < IN_SUBTILES else None, finish(s))


def _in_proj(x, g, w_in, cos_t, sin_t, wpool_bd, pscale):
    b, s, _ = x.shape
    tm = TOKEN_TILE
    row = lambda bi, i: (bi, i, 0)
    tab = lambda bi, i: (i, 0)
    stream = lambda bi, i: (bi, 0, 0, i, 0)
    stream_specs = [pl.BlockSpec((1, N_PAIRS, d, tm // d, LANES), stream) for d in DILATIONS] * 3
    stream_shapes = [_stream_shape(b, N_PAIRS, s, d, _BF16) for d in DILATIONS] * 3
    outs = pl.pallas_call(
        _in_proj_kernel,
        grid=(b, s // tm),
        in_specs=[
            pl.BlockSpec((1, tm, D_MODEL), row),
            _resident((1, D_MODEL)),
            _resident((D_MODEL, IN_WIDTH)),
            pl.BlockSpec((tm, LANES), tab),
            pl.BlockSpec((tm, LANES), tab),
            _resident((POOL_WIDTH, POOL_WIDTH)),
            _resident((1, POOL_WIDTH)),
        ],
        out_specs=[pl.BlockSpec((1, tm, POOL_WIDTH), row)] + stream_specs,
        out_shape=[jax.ShapeDtypeStruct((b, s, POOL_WIDTH), _BF16)] + stream_shapes,
        scratch_shapes=[
            pltpu.VMEM((MAX_POOL_WIN, POOL_WIDTH), _F32),
            pltpu.VMEM((N_STAGING, tm // IN_SUBTILES, LANES), _F32),
        ],
        compiler_params=pltpu.CompilerParams(
            dimension_semantics=("arbitrary", "arbitrary"), vmem_limit_bytes=VMEM_LIMIT),
        name="in_proj",
    )(x, g, w_in, cos_t, sin_t, wpool_bd, pscale)
    n_dil = len(DILATIONS)
    return outs[0], outs[1:1 + n_dil], outs[1 + n_dil:1 + 2 * n_dil], outs[1 + 2 * n_dil:]


def _blk_rows(blk):
    return slice(blk * BLOCK, (blk + 1) * BLOCK)


def _attn_kernel(q_ref, k_ref, v_ref, o_ref, lse_ref, kprev, vprev, bias_ref):
    c = pl.program_id(2)
    n_pairs, dilation, rows_per_res = q_ref.shape[1:4]
    per_res = rows_per_res // BLOCK

    @pl.when(c == 0)
    def _():
        kprev[...] = jnp.zeros_like(kprev)
        vprev[...] = jnp.zeros_like(vprev)

    qi = lax.broadcasted_iota(jnp.int32, (BLOCK, 2 * BLOCK), 0)
    kj = lax.broadcasted_iota(jnp.int32, (BLOCK, 2 * BLOCK), 1)
    dist = qi + BLOCK - kj
    band = (dist >= 0) & (dist <= BLOCK)
    band_first = band & (kj >= BLOCK * (1 - jnp.minimum(c, 1)))
    bias_ref[0] = jnp.where(band_first, 0.0, -jnp.inf)
    bias_ref[1] = jnp.where(band, 0.0, -jnp.inf)
    lane = lax.broadcasted_iota(jnp.int32, (BLOCK, LANES), 1)
    low_head = lane < HEAD_DIM
    ones_block = jnp.ones((2 * BLOCK, LANES), _BF16)
    zero_bf16 = jnp.zeros((BLOCK, LANES), _BF16)

    def band_of(ref, prev, p, r, i):
        head = prev[p, _blk_rows(r), :] if i == 0 else ref[0, p, r, _blk_rows(i - 1), :]
        return jnp.concatenate([head, ref[0, p, r, _blk_rows(i), :]], axis=0)

    def scores(p, r, i):
        q = q_ref[0, p, r, _blk_rows(i), :]
        q2 = jnp.concatenate([jnp.where(low_head, q, zero_bf16),
                              jnp.where(low_head, zero_bf16, q)], axis=0)
        return lax.dot_general(q2, band_of(k_ref, kprev, p, r, i), (((1,), (1,)), ((), ())),
                               preferred_element_type=_F32)

    def attend(p, r, i, sc):
        bias = bias_ref.at[0 if i == 0 else 1]
        maxes, probs = [], []
        for hh in range(HEADS_PER_PAIR):
            s_h = sc[hh * BLOCK:(hh + 1) * BLOCK, :] + bias[...]
            m = jnp.max(s_h, axis=-1, keepdims=True)
            maxes.append(m)
            probs.append(jnp.exp2(s_h - m).astype(_BF16))
        v_aug = jnp.concatenate([band_of(v_ref, vprev, p, r, i), ones_block], axis=1)
        pv = jnp.dot(jnp.concatenate(probs, axis=0), v_aug, preferred_element_type=_F32)
        num = jnp.where(low_head, pv[0:BLOCK, 0:LANES], pv[BLOCK:, 0:LANES])
        den = jnp.where(low_head, pv[0:BLOCK, LANES:], pv[BLOCK:, LANES:])
        o_ref[0, p, r, _blk_rows(i), :] = (num * (1.0 / den)).astype(o_ref.dtype)
        lse = jnp.where(low_head, maxes[0], maxes[1]) * LN2 + jnp.log(den)
        head_lanes = (lane % HEAD_DIM) == p
        if p == 0:
            lse_ref[0, 0, r, _blk_rows(i), :] = jnp.where(head_lanes, lse, 0.0)
        else:
            lse_ref[0, 0, r, _blk_rows(i), :] = jnp.where(head_lanes, lse,
                                                          lse_ref[0, 0, r, _blk_rows(i), :])

    work = [(p, r, i) for p in range(n_pairs) for r in range(dilation) for i in range(per_res)]
    sc_next = scores(*work[0])
    for n, item in enumerate(work):
        sc = sc_next
        if n + 1 < len(work):
            sc_next = scores(*work[n + 1])
        attend(*item, sc)
    for p in range(n_pairs):
        for r in range(dilation):
            kprev[p, _blk_rows(r), :] = k_ref[0, p, r, _blk_rows(per_res - 1), :]
            vprev[p, _blk_rows(r), :] = v_ref[0, p, r, _blk_rows(per_res - 1), :]


def _attention_config(q, k, v):
    b, _, dilation, length, _ = q.shape
    rows = ATTN_CHUNK // dilation
    pg = PAIRS_PER_GROUP
    stream = lambda bi, g, c: (bi, g, 0, c, 0)
    return pl.pallas_call(
        _attn_kernel,
        grid=(b, PAIR_GROUPS, length // rows),
        in_specs=[pl.BlockSpec((1, pg, dilation, rows, LANES), stream)] * 3,
        out_specs=(
            pl.BlockSpec((1, pg, dilation, rows, LANES), stream),
            pl.BlockSpec((1, 1, dilation, rows, LANES), stream),
        ),
        out_shape=(
            _stream_shape(b, N_PAIRS, dilation * length, dilation, _BF16),
            _stream_shape(b, PAIR_GROUPS, dilation * length, dilation, _F32),
        ),
        scratch_shapes=[
            pltpu.VMEM((pg, dilation * BLOCK, LANES), _BF16),
            pltpu.VMEM((pg, dilation * BLOCK, LANES), _BF16),
            pltpu.VMEM((2, BLOCK, 2 * BLOCK), _F32),
        ],
        compiler_params=pltpu.CompilerParams(
            dimension_semantics=("arbitrary", "arbitrary", "arbitrary"),
            vmem_limit_bytes=VMEM_LIMIT),
        name=f"attn_d{dilation}",
    )(q, k, v)


def _position_order(slab_ref, st_a, st_b):
    dilation, per_res, _ = slab_ref.shape
    if dilation == 1:
        return slab_ref[0].astype(_F32)
    if dilation == FAST_STRIDE:
        for r in range(dilation):
            st_a[pl.ds(r, per_res, stride=FAST_STRIDE), :] = slab_ref[r].astype(_F32)
        return st_a[...]
    quarter = dilation * per_res // FAST_STRIDE
    for r in range(dilation):
        a, b = r % FAST_STRIDE, r // FAST_STRIDE
        st_b[pl.ds(a * quarter + b, per_res, stride=FAST_STRIDE), :] = slab_ref[r].astype(_F32)
    for a in range(FAST_STRIDE):
        st_a[pl.ds(a, quarter, stride=FAST_STRIDE), :] = st_b[a * quarter:(a + 1) * quarter, :]
    return st_a[...]


def _merge_stages(x_ref, pool_ref, o_refs, l_refs, expand_ref, w_ref, staging, result):
    tm = x_ref.shape[1]
    lane = lax.broadcasted_iota(jnp.int32, (tm, LANES), 1)
    buffers = itertools.count()

    def ordered(slab_ref):
        j = next(buffers) % (N_STAGING // 2)
        return _position_order(slab_ref, staging.at[2 * j], staging.at[2 * j + 1])

    def head_lse(l_ref):
        out = ordered(l_ref.at[0, 0])
        for g in range(1, PAIR_GROUPS):
            shifted = pltpu.roll(ordered(l_ref.at[0, g]), g * PAIRS_PER_GROUP, axis=1)
            out = jnp.where((lane % HEAD_DIM) < g * PAIRS_PER_GROUP, out, shifted)
        return out

    l1, l2, l3 = (head_lse(l_ref) for l_ref in l_refs)
    mx = jnp.maximum(jnp.maximum(l1, l2), l3)
    e1, e2, e3 = jnp.exp(l1 - mx), jnp.exp(l2 - mx), jnp.exp(l3 - mx)
    inv = 1.0 / (e1 + e2 + e3)
    weights = [(e * inv).astype(_BF16) for e in (e1, e2)]
    yield
    wexps = [jnp.dot(w, expand_ref[...], preferred_element_type=_F32) for w in weights]
    wexps.append(1.0 - wexps[0] - wexps[1])
    mix_pool = jnp.dot(pool_ref[0], w_ref[0:POOL_WIDTH, :], preferred_element_type=_F32)
    yield
    attn = None
    for wexp, o_ref in zip(wexps, o_refs):
        o = jnp.concatenate([ordered(o_ref.at[0, p]) for p in range(N_PAIRS)], axis=1)
        term = wexp * o
        attn = term if attn is None else attn + term
        yield
    result["mix"] = mix_pool + jnp.dot(attn.astype(_BF16), w_ref[POOL_WIDTH:, :],
                                       preferred_element_type=_F32)


def _gate_up_stages(h_early_ref, h_late_ref, wg_ref, wu_ref, act_ref):
    for c in range(D_FF // FF_CHUNK):
        h_ref = h_early_ref if c < H_EARLY_CHUNKS else h_late_ref
        cols = slice(c * FF_CHUNK, (c + 1) * FF_CHUNK)
        gate = jnp.dot(h_ref[...], wg_ref[:, cols], preferred_element_type=_F32)
        up = jnp.dot(h_ref[...], wu_ref[:, cols], preferred_element_type=_F32)
        act_ref[:, cols] = (gate * jax.nn.sigmoid(gate) * up).astype(act_ref.dtype)
        yield


def _post_kernel(x_ref, pool_ref, o1_ref, o2_ref, o3_ref, l1_ref, l2_ref, l3_ref, expand_ref,
                 wout_ref, gmix_ref, gpre_ref, wg_ref, wu_ref, wd_ref, gpost_ref, out_ref,
                 act_ref, x1_stage, h_stage, x1_prev, h_prev, staging):
    t = pl.program_id(0)

    @pl.when(t == 0)
    def _():
        x1_stage[...] = jnp.zeros_like(x1_stage)
        h_stage[...] = jnp.zeros_like(h_stage)

    def stage_tile():
        x1_prev[...] = x1_stage[...]
        yield
        h_prev[...] = h_stage[...]
        yield
        result = {}
        yield from _merge_stages(x_ref, pool_ref, (o1_ref, o2_ref, o3_ref), (l1_ref, l2_ref, l3_ref),
                                 expand_ref, wout_ref, staging, result)
        yield
        x1 = x_ref[0] + _rms_scale(result["mix"]) * gmix_ref[...]
        x1_stage[...] = x1
        h_stage[...] = (_rms_scale(x1) * gpre_ref[...]).astype(_BF16)

    _emit_interleaved(_gate_up_stages(h_stage, h_prev, wg_ref, wu_ref, act_ref), stage_tile())
    f = jnp.dot(act_ref[...], wd_ref[...], preferred_element_type=_F32)
    out_ref[0] = x1_prev[...] + _rms_scale(f) * gpost_ref[...]


def _post(x, pool, outs, lses, expand, w_out, gmix, gpre, wg, wu, wd, gpost):
    b, s, _ = x.shape
    tm = POST_TILE
    per_seq = s // tm
    n_tiles = b * per_seq
    merged = lambda t: jnp.minimum(t, n_tiles - 1)
    written = lambda t: jnp.maximum(t - 1, 0)
    row = lambda t: (merged(t) // per_seq, merged(t) % per_seq, 0)
    stream = lambda t: (merged(t) // per_seq, 0, 0, merged(t) % per_seq, 0)
    out_row = lambda t: (written(t) // per_seq, written(t) % per_seq, 0)
    return pl.pallas_call(
        _post_kernel,
        grid=(n_tiles + 1,),
        in_specs=[
            pl.BlockSpec((1, tm, D_MODEL), row),
            pl.BlockSpec((1, tm, POOL_WIDTH), row),
        ] + [pl.BlockSpec((1, N_PAIRS, d, tm // d, LANES), stream) for d in DILATIONS]
          + [pl.BlockSpec((1, PAIR_GROUPS, d, tm // d, LANES), stream) for d in DILATIONS] + [
            _resident((LANES, ATTN_WIDTH)),
            _resident((D_MODEL, D_MODEL)),
            _resident((1, D_MODEL)),
            _resident((1, D_MODEL)),
            _resident((D_MODEL, D_FF)),
            _resident((D_MODEL, D_FF)),
            _resident((D_FF, D_MODEL)),
            _resident((1, D_MODEL)),
        ],
        out_specs=pl.BlockSpec((1, tm, D_MODEL), out_row),
        out_shape=jax.ShapeDtypeStruct((b, s, D_MODEL), _F32),
        scratch_shapes=[
            pltpu.VMEM((tm, D_FF), _BF16),
            pltpu.VMEM((tm, D_MODEL), _F32),
            pltpu.VMEM((tm, D_MODEL), _BF16),
            pltpu.VMEM((tm, D_MODEL), _F32),
            pltpu.VMEM((tm, D_MODEL), _BF16),
            pltpu.VMEM((N_STAGING, tm, LANES), _F32),
        ],
        compiler_params=pltpu.CompilerParams(
            dimension_semantics=("arbitrary",), vmem_limit_bytes=VMEM_LIMIT),
        name="post",
    )(x, pool, *outs, *lses, expand, w_out, gmix, gpre, wg, wu, wd, gpost)


def _rope_tables(seq_len):
    freqs = ROPE_THETA ** (-jnp.arange(HALF_DIM, dtype=_F32) * (2.0 / HEAD_DIM))
    ang = jnp.arange(seq_len).astype(_F32)[:, None] * freqs[None, :]
    cos, sin = jnp.cos(ang), jnp.sin(ang)
    cos_t = jnp.concatenate([cos, cos, cos, cos], axis=1)
    sin_t = jnp.concatenate([-sin, sin, -sin, sin], axis=1)
    return cos_t, sin_t


def _block_diag(w_pool):
    n_g = w_pool.shape[0]
    out = jnp.zeros((POOL_WIDTH, POOL_WIDTH), w_pool.dtype)
    for gi in range(n_g):
        out = out.at[gi * POOL_GROUP:(gi + 1) * POOL_GROUP,
                     gi * POOL_GROUP:(gi + 1) * POOL_GROUP].set(w_pool[gi])
    return out


def _head_expand():
    e = np.zeros((LANES, ATTN_WIDTH), np.float32)
    for p in range(N_PAIRS):
        for hh in range(HEADS_PER_PAIR):
            h = HEADS_PER_PAIR * p + hh
            e[hh * HEAD_DIM + p, h * HEAD_DIM:(h + 1) * HEAD_DIM] = 1.0
    return jnp.asarray(e, dtype=_BF16)


def kernel(x, ln_pre_mix, w_in, w_pool, pool_scale, w_out, ln_post_mix, ln_pre_ffn,
           w_gate, w_up, w_down, ln_post_ffn):
    b, s, _ = x.shape
    depth = w_in.shape[0]
    cos_t, sin_t = _rope_tables(s)
    expand = _head_expand()
    for l in range(depth):
        pool, qs, ks, vs = _in_proj(
            x, ln_pre_mix[l][None, :], w_in[l].astype(_BF16), cos_t, sin_t,
            _block_diag(w_pool[l]).astype(_BF16), pool_scale[l][None, :])
        outs, lses = [], []
        for q, k, v in zip(qs, ks, vs):
            o, lse = _attention_config(q, k, v)
            outs.append(o)
            lses.append(lse)
        x = _post(x, pool, outs, lses, expand, w_out[l].astype(_BF16), ln_post_mix[l][None, :],
                  ln_pre_ffn[l][None, :], w_gate[l].astype(_BF16), w_up[l].astype(_BF16),
                  w_down[l].astype(_BF16), ln_post_ffn[l][None, :])
    return x
```

```python
import itertools
import math

import jax
import jax.numpy as jnp
import numpy as np
from jax import lax
from jax.experimental import pallas as pl
from jax.experimental.pallas import tpu as pltpu

D_MODEL = 1024
POOL_WIDTH = 256
POOL_WINDOWS = (2, 4, 8, 16)
POOL_GROUP = 64
HEAD_DIM = 64
HALF_DIM = HEAD_DIM // 2
ATTN_WIDTH = 768
N_HEADS = 12
DILATED_CONFIGS = ((128, 1), (512, 4), (2048, 16))
DILATIONS = tuple(d for _, d in DILATED_CONFIGS)
BLOCK = 128
ROPE_THETA = 10000.0
IN_WIDTH = POOL_WIDTH + 3 * ATTN_WIDTH
D_FF = 2816
EPS = 1e-6

LANES = 128
V7X_VMEM_BYTES = 64 * 1024 * 1024
MAX_POOL_WIN = max(POOL_WINDOWS)

N_PAIRS = ATTN_WIDTH // LANES
HEADS_PER_PAIR = LANES // HEAD_DIM

TOKEN_TILE = 1024
IN_SUBTILES = 4
POST_TILE = 512
ATTN_CHUNK = 2048
PAIR_GROUPS = 1
PAIRS_PER_GROUP = N_PAIRS // PAIR_GROUPS
FAST_STRIDE = 4
N_STAGING = 12
FF_CHUNK = 256
H_EARLY_CHUNKS = 3
VMEM_LIMIT = V7X_VMEM_BYTES * 7 // 8

LOG2E = math.log2(math.e)
LN2 = math.log(2.0)

_BF16 = jnp.bfloat16
_F32 = jnp.float32


def _rms_scale(x):
    return x * lax.rsqrt(jnp.mean(x * x, axis=-1, keepdims=True) + EPS)


def _resident(shape):
    nd = len(shape)
    return pl.BlockSpec(shape, lambda *_: (0,) * nd, pipeline_mode=pl.Buffered(1))


def _emit_interleaved(main, side):
    live = [g for g in (main, side) if g is not None]
    while live:
        for g in list(live):
            if next(g, StopIteration) is StopIteration:
                live.remove(g)


def _stream_shape(batch, lead, seq_len, dilation, dtype):
    return jax.ShapeDtypeStruct((batch, lead, dilation, seq_len // dilation, LANES), dtype)


def _rope_slab(xc, cos, sin_signed, first_half):
    ahead = pltpu.roll(xc, LANES - HALF_DIM, axis=1)
    behind = pltpu.roll(xc, HALF_DIM, axis=1)
    partner = jnp.where(first_half, ahead, behind)
    return xc * cos + partner * sin_signed


def _pool_mixer(u, first_pos, halo_ref, wpool_ref, pscale_ref):
    n = u.shape[0]
    ext = jnp.concatenate([halo_ref[...], u], axis=0)
    halo_ref[...] = u[n - MAX_POOL_WIN:, :]
    s2 = ext[1:, :] + ext[:-1, :]
    s4 = s2[2:, :] + s2[:-2, :]
    s8 = s4[4:, :] + s4[:-4, :]
    s16 = s8[8:, :] + s8[:-8, :]
    w2 = s2[MAX_POOL_WIN - 1:MAX_POOL_WIN - 1 + n, :]
    w4 = s4[MAX_POOL_WIN - 3:MAX_POOL_WIN - 3 + n, :]
    w8 = s8[MAX_POOL_WIN - 7:MAX_POOL_WIN - 7 + n, :]
    w16 = s16[1:1 + n, :]
    lane = lax.broadcasted_iota(jnp.int32, (n, POOL_WIDTH), 1)
    grp = lane // POOL_GROUP
    wsum = jnp.where(grp == 0, w2, jnp.where(grp == 1, w4, jnp.where(grp == 2, w8, w16)))
    win = jnp.where(grp == 0, POOL_WINDOWS[0], jnp.where(grp == 1, POOL_WINDOWS[1],
                    jnp.where(grp == 2, POOL_WINDOWS[2], POOL_WINDOWS[3])))
    pos = first_pos + lax.broadcasted_iota(jnp.int32, (n, POOL_WIDTH), 0)
    cnt = jnp.minimum(pos + 1, win).astype(_F32)
    d = wsum / cnt - u
    y = jnp.dot(d.astype(_BF16), wpool_ref[...], preferred_element_type=_F32)
    return y * pscale_ref[...]


def _store_streams(t, st_a, st_b, out_refs, p, s, n):
    ref1, ref4, ref16 = out_refs
    ref1[0, p, 0, s * n:(s + 1) * n, :] = t.astype(ref1.dtype)
    st_a[...] = t
    quarter = n // FAST_STRIDE
    for a in range(FAST_STRIDE):
        rows_a = st_a[pl.ds(a, quarter, stride=FAST_STRIDE), :]
        ref4[0, p, a, s * quarter:(s + 1) * quarter, :] = rows_a.astype(ref4.dtype)
        st_b[a * quarter:(a + 1) * quarter, :] = rows_a
    per_res = n // DILATIONS[2]
    for r in range(DILATIONS[2]):
        a, b = r % FAST_STRIDE, r // FAST_STRIDE
        ref16[0, p, r, s * per_res:(s + 1) * per_res, :] = (
            st_b[pl.ds(a * quarter + b, per_res, stride=FAST_STRIDE), :].astype(ref16.dtype))


def _project_stages(rows, x_ref, g_ref, w_ref, proj):
    h = (_rms_scale(x_ref[0, rows, :]) * g_ref[...]).astype(_BF16)
    yield
    edges = (0, POOL_WIDTH, POOL_WIDTH + ATTN_WIDTH, POOL_WIDTH + 2 * ATTN_WIDTH, IN_WIDTH)
    for name, lo, hi in zip(("u", "q", "k", "v"), edges[:-1], edges[1:]):
        proj[name] = jnp.dot(h, w_ref[:, lo:hi], preferred_element_type=_F32)
        yield


def _finish_stages(s, rows, first_pos, proj, cos_ref, sin_ref, wpool_ref, pscale_ref, halo_ref,
                   pool_ref, q_refs, k_refs, v_refs, staging):
    n = rows.stop - rows.start
    pool_ref[0, rows, :] = _pool_mixer(proj["u"], first_pos, halo_ref, wpool_ref,
                                       pscale_ref).astype(pool_ref.dtype)
    yield
    cos = cos_ref[rows, :]
    sin_signed = sin_ref[rows, :]
    lane128 = lax.broadcasted_iota(jnp.int32, (n, LANES), 1)
    first_half = (lane128 % HEAD_DIM) < HALF_DIM
    q_scale = (HEAD_DIM ** -0.5) * LOG2E
    slab_count = 0
    for name, refs, scale in (("q", q_refs, q_scale), ("k", k_refs, None), ("v", v_refs, None)):
        for p in range(N_PAIRS):
            t = proj[name][:, p * LANES:(p + 1) * LANES]
            if name != "v":
                t = _rope_slab(t, cos, sin_signed, first_half)
            if scale is not None:
                t = t * scale
            j = slab_count % (N_STAGING // 2)
            _store_streams(t, staging.at[2 * j], staging.at[2 * j + 1], refs, p, s, n)
            slab_count += 1
            if p % 2 == 1:
                yield


def _in_proj_kernel(x_ref, g_ref, w_ref, cos_ref, sin_ref, wpool_ref, pscale_ref, pool_ref, *rest):
    n_dil = len(DILATIONS)
    q_refs, k_refs, v_refs = rest[0:n_dil], rest[n_dil:2 * n_dil], rest[2 * n_dil:3 * n_dil]
    halo_ref, staging = rest[3 * n_dil:]
    i = pl.program_id(1)
    tm = x_ref.shape[1]
    sub = tm // IN_SUBTILES
    rows = [slice(s * sub, (s + 1) * sub) for s in range(IN_SUBTILES)]
    projs = [{} for _ in range(IN_SUBTILES)]

    @pl.when(i == 0)
    def _():
        halo_ref[...] = jnp.zeros_like(halo_ref)

    def project(s):
        return _project_stages(rows[s], x_ref, g_ref, w_ref, projs[s])

    def finish(s):
        return _finish_stages(s, rows[s], i * tm + s * sub, projs[s], cos_ref, sin_ref, wpool_ref,
                              pscale_ref, halo_ref, pool_ref, q_refs, k_refs, v_refs, staging)

    _emit_interleaved(project(0), None)
    for s in range(IN_SUBTILES):
        _emit_interleaved(project(s + 1) if s + 1 < IN_SUBTILES else None, finish(s))


def _in_proj(x, g, w_in, cos_t, sin_t, wpool_bd, pscale):
    b, s, _ = x.shape
    tm = TOKEN_TILE
    row = lambda bi, i: (bi, i, 0)
    tab = lambda bi, i: (i, 0)
    stream = lambda bi, i: (bi, 0, 0, i, 0)
    stream_specs = [pl.BlockSpec((1, N_PAIRS, d, tm // d, LANES), stream) for d in DILATIONS] * 3
    stream_shapes = [_stream_shape(b, N_PAIRS, s, d, _BF16) for d in DILATIONS] * 3
    outs = pl.pallas_call(
        _in_proj_kernel,
        grid=(b, s // tm),
        in_specs=[
            pl.BlockSpec((1, tm, D_MODEL), row),
            _resident((1, D_MODEL)),
            _resident((D_MODEL, IN_WIDTH)),
            pl.BlockSpec((tm, LANES), tab),
            pl.BlockSpec((tm, LANES), tab),
            _resident((POOL_WIDTH, POOL_WIDTH)),
            _resident((1, POOL_WIDTH)),
        ],
        out_specs=[pl.BlockSpec((1, tm, POOL_WIDTH), row)] + stream_specs,
        out_shape=[jax.ShapeDtypeStruct((b, s, POOL_WIDTH), _BF16)] + stream_shapes,
        scratch_shapes=[
            pltpu.VMEM((MAX_POOL_WIN, POOL_WIDTH), _F32),
            pltpu.VMEM((N_STAGING, tm // IN_SUBTILES, LANES), _F32),
        ],
        compiler_params=pltpu.CompilerParams(
            dimension_semantics=("arbitrary", "arbitrary"), vmem_limit_bytes=VMEM_LIMIT),
        name="in_proj",
    )(x, g, w_in, cos_t, sin_t, wpool_bd, pscale)
    n_dil = len(DILATIONS)
    return outs[0], outs[1:1 + n_dil], outs[1 + n_dil:1 + 2 * n_dil], outs[1 + 2 * n_dil:]


def _blk_rows(blk):
    return slice(blk * BLOCK, (blk + 1) * BLOCK)


def _attn_kernel(q_ref, k_ref, v_ref, o_ref, lse_ref, kprev, vprev, bias_ref):
    c = pl.program_id(2)
    n_pairs, dilation, rows_per_res = q_ref.shape[1:4]
    per_res = rows_per_res // BLOCK

    @pl.when(c == 0)
    def _():
        kprev[...] = jnp.zeros_like(kprev)
        vprev[...] = jnp.zeros_like(vprev)

    qi = lax.broadcasted_iota(jnp.int32, (BLOCK, 2 * BLOCK), 0)
    kj = lax.broadcasted_iota(jnp.int32, (BLOCK, 2 * BLOCK), 1)
    dist = qi + BLOCK - kj
    band = (dist >= 0) & (dist <= BLOCK)
    band_first = band & (kj >= BLOCK * (1 - jnp.minimum(c, 1)))
    bias_ref[0] = jnp.where(band_first, 0.0, -jnp.inf)
    bias_ref[1] = jnp.where(band, 0.0, -jnp.inf)
    lane = lax.broadcasted_iota(jnp.int32, (BLOCK, LANES), 1)
    low_head = lane < HEAD_DIM
    ones_block = jnp.ones((2 * BLOCK, LANES), _BF16)
    zero_bf16 = jnp.zeros((BLOCK, LANES), _BF16)

    def band_of(ref, prev, p, r, i):
        head = prev[p, _blk_rows(r), :] if i == 0 else ref[0, p, r, _blk_rows(i - 1), :]
        return jnp.concatenate([head, ref[0, p, r, _blk_rows(i), :]], axis=0)

    def scores(p, r, i):
        q = q_ref[0, p, r, _blk_rows(i), :]
        q2 = jnp.concatenate([jnp.where(low_head, q, zero_bf16),
                              jnp.where(low_head, zero_bf16, q)], axis=0)
        return lax.dot_general(q2, band_of(k_ref, kprev, p, r, i), (((1,), (1,)), ((), ())),
                               preferred_element_type=_F32)

    def attend(p, r, i, sc):
        bias = bias_ref.at[0 if i == 0 else 1]
        maxes, probs = [], []
        for hh in range(HEADS_PER_PAIR):
            s_h = sc[hh * BLOCK:(hh + 1) * BLOCK, :] + bias[...]
            m = jnp.max(s_h, axis=-1, keepdims=True)
            maxes.append(m)
            probs.append(jnp.exp2(s_h - m).astype(_BF16))
        v_aug = jnp.concatenate([band_of(v_ref, vprev, p, r, i), ones_block], axis=1)
        pv = jnp.dot(jnp.concatenate(probs, axis=0), v_aug, preferred_element_type=_F32)
        num = jnp.where(low_head, pv[0:BLOCK, 0:LANES], pv[BLOCK:, 0:LANES])
        den = jnp.where(low_head, pv[0:BLOCK, LANES:], pv[BLOCK:, LANES:])
        o_ref[0, p, r, _blk_rows(i), :] = (num * (1.0 / den)).astype(o_ref.dtype)
        lse = jnp.where(low_head, maxes[0], maxes[1]) * LN2 + jnp.log(den)
        head_lanes = (lane % HEAD_DIM) == p
        if p == 0:
            lse_ref[0, 0, r, _blk_rows(i), :] = jnp.where(head_lanes, lse, 0.0)
        else:
            lse_ref[0, 0, r, _blk_rows(i), :] = jnp.where(head_lanes, lse,
                                                          lse_ref[0, 0, r, _blk_rows(i), :])

    work = [(p, r, i) for p in range(n_pairs) for r in range(dilation) for i in range(per_res)]
    sc_next = scores(*work[0])
    for n, item in enumerate(work):
        sc = sc_next
        if n + 1 < len(work):
            sc_next = scores(*work[n + 1])
        attend(*item, sc)
    for p in range(n_pairs):
        for r in range(dilation):
            kprev[p, _blk_rows(r), :] = k_ref[0, p, r, _blk_rows(per_res - 1), :]
            vprev[p, _blk_rows(r), :] = v_ref[0, p, r, _blk_rows(per_res - 1), :]


def _attention_config(q, k, v):
    b, _, dilation, length, _ = q.shape
    rows = ATTN_CHUNK // dilation
    pg = PAIRS_PER_GROUP
    stream = lambda bi, g, c: (bi, g, 0, c, 0)
    return pl.pallas_call(
        _attn_kernel,
        grid=(b, PAIR_GROUPS, length // rows),
        in_specs=[pl.BlockSpec((1, pg, dilation, rows, LANES), stream)] * 3,
        out_specs=(
            pl.BlockSpec((1, pg, dilation, rows, LANES), stream),
            pl.BlockSpec((1, 1, dilation, rows, LANES), stream),
        ),
        out_shape=(
            _stream_shape(b, N_PAIRS, dilation * length, dilation, _BF16),
            _stream_shape(b, PAIR_GROUPS, dilation * length, dilation, _F32),
        ),
        scratch_shapes=[
            pltpu.VMEM((pg, dilation * BLOCK, LANES), _BF16),
            pltpu.VMEM((pg, dilation * BLOCK, LANES), _BF16),
            pltpu.VMEM((2, BLOCK, 2 * BLOCK), _F32),
        ],
        compiler_params=pltpu.CompilerParams(
            dimension_semantics=("arbitrary", "arbitrary", "arbitrary"),
            vmem_limit_bytes=VMEM_LIMIT),
        name=f"attn_d{dilation}",
    )(q, k, v)


def _position_order(slab_ref, st_a, st_b):
    dilation, per_res, _ = slab_ref.shape
    if dilation == 1:
        return slab_ref[0].astype(_F32)
    if dilation == FAST_STRIDE:
        for r in range(dilation):
            st_a[pl.ds(r, per_res, stride=FAST_STRIDE), :] = slab_ref[r].astype(_F32)
        return st_a[...]
    quarter = dilation * per_res // FAST_STRIDE
    for r in range(dilation):
        a, b = r % FAST_STRIDE, r // FAST_STRIDE
        st_b[pl.ds(a * quarter + b, per_res, stride=FAST_STRIDE), :] = slab_ref[r].astype(_F32)
    for a in range(FAST_STRIDE):
        st_a[pl.ds(a, quarter, stride=FAST_STRIDE), :] = st_b[a * quarter:(a + 1) * quarter, :]
    return st_a[...]


def _merge_stages(x_ref, pool_ref, o_refs, l_refs, expand_ref, w_ref, staging, result):
    tm = x_ref.shape[1]
    lane = lax.broadcasted_iota(jnp.int32, (tm, LANES), 1)
    buffers = itertools.count()

    def ordered(slab_ref):
        j = next(buffers) % (N_STAGING // 2)
        return _position_order(slab_ref, staging.at[2 * j], staging.at[2 * j + 1])

    def head_lse(l_ref):
        out = ordered(l_ref.at[0, 0])
        for g in range(1, PAIR_GROUPS):
            shifted = pltpu.roll(ordered(l_ref.at[0, g]), g * PAIRS_PER_GROUP, axis=1)
            out = jnp.where((lane % HEAD_DIM) < g * PAIRS_PER_GROUP, out, shifted)
        return out

    l1, l2, l3 = (head_lse(l_ref) for l_ref in l_refs)
    mx = jnp.maximum(jnp.maximum(l1, l2), l3)
    e1, e2, e3 = jnp.exp(l1 - mx), jnp.exp(l2 - mx), jnp.exp(l3 - mx)
    inv = 1.0 / (e1 + e2 + e3)
    weights = [(e * inv).astype(_BF16) for e in (e1, e2)]
    yield
    wexps = [jnp.dot(w, expand_ref[...], preferred_element_type=_F32) for w in weights]
    wexps.append(1.0 - wexps[0] - wexps[1])
    mix_pool = jnp.dot(pool_ref[0], w_ref[0:POOL_WIDTH, :], preferred_element_type=_F32)
    yield
    attn = None
    for wexp, o_ref in zip(wexps, o_refs):
        o = jnp.concatenate([ordered(o_ref.at[0, p]) for p in range(N_PAIRS)], axis=1)
        term = wexp * o
        attn = term if attn is None else attn + term
        yield
    result["mix"] = mix_pool + jnp.dot(attn.astype(_BF16), w_ref[POOL_WIDTH:, :],
                                       preferred_element_type=_F32)


def _gate_up_stages(h_early_ref, h_late_ref, wg_ref, wu_ref, act_ref):
    for c in range(D_FF // FF_CHUNK):
        h_ref = h_early_ref if c < H_EARLY_CHUNKS else h_late_ref
        cols = slice(c * FF_CHUNK, (c + 1) * FF_CHUNK)
        gate = jnp.dot(h_ref[...], wg_ref[:, cols], preferred_element_type=_F32)
        up = jnp.dot(h_ref[...], wu_ref[:, cols], preferred_element_type=_F32)
        act_ref[:, cols] = (gate * jax.nn.sigmoid(gate) * up).astype(act_ref.dtype)
        yield


def _post_kernel(x_ref, pool_ref, o1_ref, o2_ref, o3_ref, l1_ref, l2_ref, l3_ref, expand_ref,
                 wout_ref, gmix_ref, gpre_ref, wg_ref, wu_ref, wd_ref, gpost_ref, out_ref,
                 act_ref, x1_stage, h_stage, x1_prev, h_prev, staging):
    t = pl.program_id(0)

    @pl.when(t == 0)
    def _():
        x1_stage[...] = jnp.zeros_like(x1_stage)
        h_stage[...] = jnp.zeros_like(h_stage)

    def stage_tile():
        x1_prev[...] = x1_stage[...]
        yield
        h_prev[...] = h_stage[...]
        yield
        result = {}
        yield from _merge_stages(x_ref, pool_ref, (o1_ref, o2_ref, o3_ref), (l1_ref, l2_ref, l3_ref),
                                 expand_ref, wout_ref, staging, result)
        yield
        x1 = x_ref[0] + _rms_scale(result["mix"]) * gmix_ref[...]
        x1_stage[...] = x1
        h_stage[...] = (_rms_scale(x1) * gpre_ref[...]).astype(_BF16)

    _emit_interleaved(_gate_up_stages(h_stage, h_prev, wg_ref, wu_ref, act_ref), stage_tile())
    f = jnp.dot(act_ref[...], wd_ref[...], preferred_element_type=_F32)
    out_ref[0] = x1_prev[...] + _rms_scale(f) * gpost_ref[...]


def _post(x, pool, outs, lses, expand, w_out, gmix, gpre, wg, wu, wd, gpost):
    b, s, _ = x.shape
    tm = POST_TILE
    per_seq = s // tm
    n_tiles = b * per_seq
    merged = lambda t: jnp.minimum(t, n_tiles - 1)
    written = lambda t: jnp.maximum(t - 1, 0)
    row = lambda t: (merged(t) // per_seq, merged(t) % per_seq, 0)
    stream = lambda t: (merged(t) // per_seq, 0, 0, merged(t) % per_seq, 0)
    out_row = lambda t: (written(t) // per_seq, written(t) % per_seq, 0)
    return pl.pallas_call(
        _post_kernel,
        grid=(n_tiles + 1,),
        in_specs=[
            pl.BlockSpec((1, tm, D_MODEL), row),
            pl.BlockSpec((1, tm, POOL_WIDTH), row),
        ] + [pl.BlockSpec((1, N_PAIRS, d, tm // d, LANES), stream) for d in DILATIONS]
          + [pl.BlockSpec((1, PAIR_GROUPS, d, tm // d, LANES), stream) for d in DILATIONS] + [
            _resident((LANES, ATTN_WIDTH)),
            _resident((D_MODEL, D_MODEL)),
            _resident((1, D_MODEL)),
            _resident((1, D_MODEL)),
            _resident((D_MODEL, D_FF)),
            _resident((D_MODEL, D_FF)),
            _resident((D_FF, D_MODEL)),
            _resident((1, D_MODEL)),
        ],
        out_specs=pl.BlockSpec((1, tm, D_MODEL), out_row),
        out_shape=jax.ShapeDtypeStruct((b, s, D_MODEL), _F32),
        scratch_shapes=[
            pltpu.VMEM((tm, D_FF), _BF16),
            pltpu.VMEM((tm, D_MODEL), _F32),
            pltpu.VMEM((tm, D_MODEL), _BF16),
            pltpu.VMEM((tm, D_MODEL), _F32),
            pltpu.VMEM((tm, D_MODEL), _BF16),
            pltpu.VMEM((N_STAGING, tm, LANES), _F32),
        ],
        compiler_params=pltpu.CompilerParams(
            dimension_semantics=("arbitrary",), vmem_limit_bytes=VMEM_LIMIT),
        name="post",
    )(x, pool, *outs, *lses, expand, w_out, gmix, gpre, wg, wu, wd, gpost)


def _rope_tables(seq_len):
    freqs = ROPE_THETA ** (-np.arange(HALF_DIM, dtype=np.float64) * (2.0 / HEAD_DIM))
    ang = np.arange(seq_len, dtype=np.float64)[:, None] * freqs[None, :]
    cos, sin = np.cos(ang), np.sin(ang)
    cos_t = np.concatenate([cos, cos, cos, cos], axis=1)
    sin_t = np.concatenate([-sin, sin, -sin, sin], axis=1)
    return jnp.asarray(cos_t, dtype=_F32), jnp.asarray(sin_t, dtype=_F32)


def _block_diag(w_pool):
    n_g = w_pool.shape[0]
    out = jnp.zeros((POOL_WIDTH, POOL_WIDTH), w_pool.dtype)
    for gi in range(n_g):
        out = out.at[gi * POOL_GROUP:(gi + 1) * POOL_GROUP,
                     gi * POOL_GROUP:(gi + 1) * POOL_GROUP].set(w_pool[gi])
    return out


def _head_expand():
    e = np.zeros((LANES, ATTN_WIDTH), np.float32)
    for p in range(N_PAIRS):
        for hh in range(HEADS_PER_PAIR):
            h = HEADS_PER_PAIR * p + hh
            e[hh * HEAD_DIM + p, h * HEAD_DIM:(h + 1) * HEAD_DIM] = 1.0
    return jnp.asarray(e, dtype=_BF16)


def kernel(x, ln_pre_mix, w_in, w_pool, pool_scale, w_out, ln_post_mix, ln_pre_ffn,
           w_gate, w_up, w_down, ln_post_ffn):
    b, s, _ = x.shape
    depth = w_in.shape[0]
    cos_t, sin_t = _rope_tables(s)
    expand = _head_expand()
    for l in range(depth):
        pool, qs, ks, vs = _in_proj(
            x, ln_pre_mix[l][None, :], w_in[l].astype(_BF16), cos_t, sin_t,
            _block_diag(w_pool[l]).astype(_BF16), pool_scale[l][None, :])
        outs, lses = [], []
        for q, k, v in zip(qs, ks, vs):
            o, lse = _attention_config(q, k, v)
            outs.append(o)
            lses.append(lse)
        x = _post(x, pool, outs, lses, expand, w_out[l].astype(_BF16), ln_post_mix[l][None, :],
                  ln_pre_ffn[l][None, :], w_gate[l].astype(_BF16), w_up[l].astype(_BF16),
                  w_down[l].astype(_BF16), ln_post_ffn[l][None, :])
    return x
```

```python
import itertools
import math

import jax
import jax.numpy as jnp
import numpy as np
from jax import lax
from jax.experimental import pallas as pl
from jax.experimental.pallas import tpu as pltpu

D_MODEL = 1024
POOL_WIDTH = 256
POOL_WINDOWS = (2, 4, 8, 16)
POOL_GROUP = 64
HEAD_DIM = 64
HALF_DIM = HEAD_DIM // 2
ATTN_WIDTH = 768
N_HEADS = 12
DILATED_CONFIGS = ((128, 1), (512, 4), (2048, 16))
DILATIONS = tuple(d for _, d in DILATED_CONFIGS)
BLOCK = 128
ROPE_THETA = 10000.0
IN_WIDTH = POOL_WIDTH + 3 * ATTN_WIDTH
D_FF = 2816
EPS = 1e-6

LANES = 128
V7X_VMEM_BYTES = 64 * 1024 * 1024
MAX_POOL_WIN = max(POOL_WINDOWS)

N_PAIRS = ATTN_WIDTH // LANES
HEADS_PER_PAIR = LANES // HEAD_DIM

TOKEN_TILE = 1024
IN_SUBTILES = 4
POST_TILE = 512
ATTN_CHUNK = 2048
PAIR_GROUPS = 1
PAIRS_PER_GROUP = N_PAIRS // PAIR_GROUPS
FAST_STRIDE = 4
N_STAGING = 12
FF_CHUNK = 256
H_EARLY_CHUNKS = 3
VMEM_LIMIT = V7X_VMEM_BYTES * 7 // 8

LOG2E = math.log2(math.e)
LN2 = math.log(2.0)

_BF16 = jnp.bfloat16
_F32 = jnp.float32


def _rms_scale(x):
    return x * lax.rsqrt(jnp.mean(x * x, axis=-1, keepdims=True) + EPS)


def _resident(shape):
    nd = len(shape)
    return pl.BlockSpec(shape, lambda *_: (0,) * nd, pipeline_mode=pl.Buffered(1))


def _emit_interleaved(main, side):
    live = [g for g in (main, side) if g is not None]
    while live:
        for g in list(live):
            if next(g, StopIteration) is StopIteration:
                live.remove(g)


def _stream_shape(batch, lead, seq_len, dilation, dtype):
    return jax.ShapeDtypeStruct((batch, lead, dilation, seq_len // dilation, LANES), dtype)


def _rope_slab(xc, cos, sin_signed, first_half):
    ahead = pltpu.roll(xc, LANES - HALF_DIM, axis=1)
    behind = pltpu.roll(xc, HALF_DIM, axis=1)
    partner = jnp.where(first_half, ahead, behind)
    return xc * cos + partner * sin_signed


def _pool_mixer(u, first_pos, halo_ref, wpool_ref, pscale_ref):
    n = u.shape[0]
    ext = jnp.concatenate([halo_ref[...], u], axis=0)
    halo_ref[...] = u[n - MAX_POOL_WIN:, :]
    s2 = ext[1:, :] + ext[:-1, :]
    s4 = s2[2:, :] + s2[:-2, :]
    s8 = s4[4:, :] + s4[:-4, :]
    s16 = s8[8:, :] + s8[:-8, :]
    w2 = s2[MAX_POOL_WIN - 1:MAX_POOL_WIN - 1 + n, :]
    w4 = s4[MAX_POOL_WIN - 3:MAX_POOL_WIN - 3 + n, :]
    w8 = s8[MAX_POOL_WIN - 7:MAX_POOL_WIN - 7 + n, :]
    w16 = s16[1:1 + n, :]
    lane = lax.broadcasted_iota(jnp.int32, (n, POOL_WIDTH), 1)
    grp = lane // POOL_GROUP
    wsum = jnp.where(grp == 0, w2, jnp.where(grp == 1, w4, jnp.where(grp == 2, w8, w16)))
    win = jnp.where(grp == 0, POOL_WINDOWS[0], jnp.where(grp == 1, POOL_WINDOWS[1],
                    jnp.where(grp == 2, POOL_WINDOWS[2], POOL_WINDOWS[3])))
    pos = first_pos + lax.broadcasted_iota(jnp.int32, (n, POOL_WIDTH), 0)
    cnt = jnp.minimum(pos + 1, win).astype(_F32)
    d = wsum / cnt - u
    y = jnp.dot(d.astype(_BF16), wpool_ref[...], preferred_element_type=_F32)
    return y * pscale_ref[...]


def _store_streams(t, st_a, st_b, out_refs, p, s, n):
    ref1, ref4, ref16 = out_refs
    ref1[0, p, 0, s * n:(s + 1) * n, :] = t.astype(ref1.dtype)
    st_a[...] = t
    quarter = n // FAST_STRIDE
    for a in range(FAST_STRIDE):
        rows_a = st_a[pl.ds(a, quarter, stride=FAST_STRIDE), :]
        ref4[0, p, a, s * quarter:(s + 1) * quarter, :] = rows_a.astype(ref4.dtype)
        st_b[a * quarter:(a + 1) * quarter, :] = rows_a
    per_res = n // DILATIONS[2]
    for r in range(DILATIONS[2]):
        a, b = r % FAST_STRIDE, r // FAST_STRIDE
        ref16[0, p, r, s * per_res:(s + 1) * per_res, :] = (
            st_b[pl.ds(a * quarter + b, per_res, stride=FAST_STRIDE), :].astype(ref16.dtype))


def _project_stages(rows, x_ref, g_ref, w_ref, proj):
    h = (_rms_scale(x_ref[0, rows, :]) * g_ref[...]).astype(_BF16)
    yield
    edges = (0, POOL_WIDTH, POOL_WIDTH + ATTN_WIDTH, POOL_WIDTH + 2 * ATTN_WIDTH, IN_WIDTH)
    for name, lo, hi in zip(("u", "q", "k", "v"), edges[:-1], edges[1:]):
        proj[name] = jnp.dot(h, w_ref[:, lo:hi], preferred_element_type=_F32)
        yield


def _finish_stages(s, rows, first_pos, proj, cos_ref, sin_ref, wpool_ref, pscale_ref, halo_ref,
                   pool_ref, q_refs, k_refs, v_refs, staging):
    n = rows.stop - rows.start
    pool_ref[0, rows, :] = _pool_mixer(proj["u"], first_pos, halo_ref, wpool_ref,
                                       pscale_ref).astype(pool_ref.dtype)
    yield
    cos = cos_ref[rows, :]
    sin_signed = sin_ref[rows, :]
    lane128 = lax.broadcasted_iota(jnp.int32, (n, LANES), 1)
    first_half = (lane128 % HEAD_DIM) < HALF_DIM
    q_scale = (HEAD_DIM ** -0.5) * LOG2E
    slab_count = 0
    for name, refs, scale in (("q", q_refs, q_scale), ("k", k_refs, None), ("v", v_refs, None)):
        for p in range(N_PAIRS):
            t = proj[name][:, p * LANES:(p + 1) * LANES]
            if name != "v":
                t = _rope_slab(t, cos, sin_signed, first_half)
            if scale is not None:
                t = t * scale
            j = slab_count % (N_STAGING // 2)
            _store_streams(t, staging.at[2 * j], staging.at[2 * j + 1], refs, p, s, n)
            slab_count += 1
            if p % 2 == 1:
                yield


def _in_proj_kernel(x_ref, g_ref, w_ref, cos_ref, sin_ref, wpool_ref, pscale_ref, pool_ref, *rest):
    n_dil = len(DILATIONS)
    q_refs, k_refs, v_refs = rest[0:n_dil], rest[n_dil:2 * n_dil], rest[2 * n_dil:3 * n_dil]
    halo_ref, staging = rest[3 * n_dil:]
    i = pl.program_id(1)
    tm = x_ref.shape[1]
    sub = tm // IN_SUBTILES
    rows = [slice(s * sub, (s + 1) * sub) for s in range(IN_SUBTILES)]
    projs = [{} for _ in range(IN_SUBTILES)]

    @pl.when(i == 0)
    def _():
        halo_ref[...] = jnp.zeros_like(halo_ref)

    def project(s):
        return _project_stages(rows[s], x_ref, g_ref, w_ref, projs[s])

    def finish(s):
        return _finish_stages(s, rows[s], i * tm + s * sub, projs[s], cos_ref, sin_ref, wpool_ref,
                              pscale_ref, halo_ref, pool_ref, q_refs, k_refs, v_refs, staging)

    _emit_interleaved(project(0), None)
    for s in range(IN_SUBTILES):
        _emit_interleaved(project(s + 1) if s + 1 < IN_SUBTILES else None, finish(s))


def _in_proj(x, g, w_in, cos_t, sin_t, wpool_bd, pscale):
    b, s, _ = x.shape
    tm = TOKEN_TILE
    row = lambda bi, i: (bi, i, 0)
    tab = lambda bi, i: (i, 0)
    stream = lambda bi, i: (bi, 0, 0, i, 0)
    stream_specs = [pl.BlockSpec((1, N_PAIRS, d, tm // d, LANES), stream) for d in DILATIONS] * 3
    stream_shapes = [_stream_shape(b, N_PAIRS, s, d, _BF16) for d in DILATIONS] * 3
    outs = pl.pallas_call(
        _in_proj_kernel,
        grid=(b, s // tm),
        in_specs=[
            pl.BlockSpec((1, tm, D_MODEL), row),
            _resident((1, D_MODEL)),
            _resident((D_MODEL, IN_WIDTH)),
            pl.BlockSpec((tm, LANES), tab),
            pl.BlockSpec((tm, LANES), tab),
            _resident((POOL_WIDTH, POOL_WIDTH)),
            _resident((1, POOL_WIDTH)),
        ],
        out_specs=[pl.BlockSpec((1, tm, POOL_WIDTH), row)] + stream_specs,
        out_shape=[jax.ShapeDtypeStruct((b, s, POOL_WIDTH), _BF16)] + stream_shapes,
        scratch_shapes=[
            pltpu.VMEM((MAX_POOL_WIN, POOL_WIDTH), _F32),
            pltpu.VMEM((N_STAGING, tm // IN_SUBTILES, LANES), _F32),
        ],
        compiler_params=pltpu.CompilerParams(
            dimension_semantics=("arbitrary", "arbitrary"), vmem_limit_bytes=VMEM_LIMIT),
        name="in_proj",
    )(x, g, w_in, cos_t, sin_t, wpool_bd, pscale)
    n_dil = len(DILATIONS)
    return outs[0], outs[1:1 + n_dil], outs[1 + n_dil:1 + 2 * n_dil], outs[1 + 2 * n_dil:]


def _blk_rows(blk):
    return slice(blk * BLOCK, (blk + 1) * BLOCK)


def _attn_kernel(q_ref, k_ref, v_ref, kprev_ref, vprev_ref, o_ref, lse_ref, bias_ref):
    c = pl.program_id(2)
    n_pairs, dilation, rows_per_res = q_ref.shape[1:4]
    per_res = rows_per_res // BLOCK

    qi = lax.broadcasted_iota(jnp.int32, (BLOCK, 2 * BLOCK), 0)
    kj = lax.broadcasted_iota(jnp.int32, (BLOCK, 2 * BLOCK), 1)
    dist = qi + BLOCK - kj
    band = (dist >= 0) & (dist <= BLOCK)
    band_first = band & (kj >= BLOCK * (1 - jnp.minimum(c, 1)))
    bias_ref[0] = jnp.where(band_first, 0.0, -jnp.inf)
    bias_ref[1] = jnp.where(band, 0.0, -jnp.inf)
    lane = lax.broadcasted_iota(jnp.int32, (BLOCK, LANES), 1)
    low_head = lane < HEAD_DIM
    ones_block = jnp.ones((2 * BLOCK, LANES), _BF16)
    zero_bf16 = jnp.zeros((BLOCK, LANES), _BF16)

    def band_of(ref, prev_ref, p, r, i):
        head = prev_ref[0, p, r] if i == 0 else ref[0, p, r, _blk_rows(i - 1), :]
        return jnp.concatenate([head, ref[0, p, r, _blk_rows(i), :]], axis=0)

    def scores(p, r, i):
        q = q_ref[0, p, r, _blk_rows(i), :]
        q2 = jnp.concatenate([jnp.where(low_head, q, zero_bf16),
                              jnp.where(low_head, zero_bf16, q)], axis=0)
        return lax.dot_general(q2, band_of(k_ref, kprev_ref, p, r, i), (((1,), (1,)), ((), ())),
                               preferred_element_type=_F32)

    def attend(p, r, i, sc):
        bias = bias_ref.at[0 if i == 0 else 1]
        maxes, probs = [], []
        for hh in range(HEADS_PER_PAIR):
            s_h = sc[hh * BLOCK:(hh + 1) * BLOCK, :] + bias[...]
            m = jnp.max(s_h, axis=-1, keepdims=True)
            maxes.append(m)
            probs.append(jnp.exp2(s_h - m).astype(_BF16))
        v_aug = jnp.concatenate([band_of(v_ref, vprev_ref, p, r, i), ones_block], axis=1)
        pv = jnp.dot(jnp.concatenate(probs, axis=0), v_aug, preferred_element_type=_F32)
        num = jnp.where(low_head, pv[0:BLOCK, 0:LANES], pv[BLOCK:, 0:LANES])
        den = jnp.where(low_head, pv[0:BLOCK, LANES:], pv[BLOCK:, LANES:])
        o_ref[0, p, r, _blk_rows(i), :] = (num * (1.0 / den)).astype(o_ref.dtype)
        lse = jnp.where(low_head, maxes[0], maxes[1]) * LN2 + jnp.log(den)
        head_lanes = (lane % HEAD_DIM) == p
        if p == 0:
            lse_ref[0, 0, r, _blk_rows(i), :] = jnp.where(head_lanes, lse, 0.0)
        else:
            lse_ref[0, 0, r, _blk_rows(i), :] = jnp.where(head_lanes, lse,
                                                          lse_ref[0, 0, r, _blk_rows(i), :])

    work = [(p, r, i) for p in range(n_pairs) for r in range(dilation) for i in range(per_res)]
    sc_next = scores(*work[0])
    for n, item in enumerate(work):
        sc = sc_next
        if n + 1 < len(work):
            sc_next = scores(*work[n + 1])
        attend(*item, sc)


def _attention_config(q, k, v):
    b, _, dilation, length, _ = q.shape
    rows = ATTN_CHUNK // dilation
    pg = PAIRS_PER_GROUP
    stream = lambda bi, g, c: (bi, g, 0, c, 0)
    before = lambda bi, g, c: (bi, g, 0, jnp.maximum(c * (rows // BLOCK) - 1, 0), 0)
    return pl.pallas_call(
        _attn_kernel,
        grid=(b, PAIR_GROUPS, length // rows),
        in_specs=[pl.BlockSpec((1, pg, dilation, rows, LANES), stream)] * 3
        + [pl.BlockSpec((1, pg, dilation, BLOCK, LANES), before)] * 2,
        out_specs=(
            pl.BlockSpec((1, pg, dilation, rows, LANES), stream),
            pl.BlockSpec((1, 1, dilation, rows, LANES), stream),
        ),
        out_shape=(
            _stream_shape(b, N_PAIRS, dilation * length, dilation, _BF16),
            _stream_shape(b, PAIR_GROUPS, dilation * length, dilation, _F32),
        ),
        scratch_shapes=[pltpu.VMEM((2, BLOCK, 2 * BLOCK), _F32)],
        compiler_params=pltpu.CompilerParams(
            dimension_semantics=("arbitrary", "arbitrary", "arbitrary"),
            vmem_limit_bytes=VMEM_LIMIT),
        name=f"attn_d{dilation}",
    )(q, k, v, k, v)


def _position_order(slab_ref, st_a, st_b):
    dilation, per_res, _ = slab_ref.shape
    if dilation == 1:
        return slab_ref[0].astype(_F32)
    if dilation == FAST_STRIDE:
        for r in range(dilation):
            st_a[pl.ds(r, per_res, stride=FAST_STRIDE), :] = slab_ref[r].astype(_F32)
        return st_a[...]
    quarter = dilation * per_res // FAST_STRIDE
    for r in range(dilation):
        a, b = r % FAST_STRIDE, r // FAST_STRIDE
        st_b[pl.ds(a * quarter + b, per_res, stride=FAST_STRIDE), :] = slab_ref[r].astype(_F32)
    for a in range(FAST_STRIDE):
        st_a[pl.ds(a, quarter, stride=FAST_STRIDE), :] = st_b[a * quarter:(a + 1) * quarter, :]
    return st_a[...]


def _merge_stages(x_ref, pool_ref, o_refs, l_refs, expand_ref, w_ref, staging, result):
    tm = x_ref.shape[1]
    lane = lax.broadcasted_iota(jnp.int32, (tm, LANES), 1)
    buffers = itertools.count()

    def ordered(slab_ref):
        j = next(buffers) % (N_STAGING // 2)
        return _position_order(slab_ref, staging.at[2 * j], staging.at[2 * j + 1])

    def head_lse(l_ref):
        out = ordered(l_ref.at[0, 0])
        for g in range(1, PAIR_GROUPS):
            shifted = pltpu.roll(ordered(l_ref.at[0, g]), g * PAIRS_PER_GROUP, axis=1)
            out = jnp.where((lane % HEAD_DIM) < g * PAIRS_PER_GROUP, out, shifted)
        return out

    l1, l2, l3 = (head_lse(l_ref) for l_ref in l_refs)
    mx = jnp.maximum(jnp.maximum(l1, l2), l3)
    e1, e2, e3 = jnp.exp(l1 - mx), jnp.exp(l2 - mx), jnp.exp(l3 - mx)
    inv = 1.0 / (e1 + e2 + e3)
    weights = [(e * inv).astype(_BF16) for e in (e1, e2)]
    yield
    wexps = [jnp.dot(w, expand_ref[...], preferred_element_type=_F32) for w in weights]
    wexps.append(1.0 - wexps[0] - wexps[1])
    mix_pool = jnp.dot(pool_ref[0], w_ref[0:POOL_WIDTH, :], preferred_element_type=_F32)
    yield
    attn = None
    for wexp, o_ref in zip(wexps, o_refs):
        o = jnp.concatenate([ordered(o_ref.at[0, p]) for p in range(N_PAIRS)], axis=1)
        term = wexp * o
        attn = term if attn is None else attn + term
        yield
    result["mix"] = mix_pool + jnp.dot(attn.astype(_BF16), w_ref[POOL_WIDTH:, :],
                                       preferred_element_type=_F32)


def _gate_up_stages(h_early_ref, h_late_ref, wg_ref, wu_ref, act_ref):
    for c in range(D_FF // FF_CHUNK):
        h_ref = h_early_ref if c < H_EARLY_CHUNKS else h_late_ref
        cols = slice(c * FF_CHUNK, (c + 1) * FF_CHUNK)
        gate = jnp.dot(h_ref[...], wg_ref[:, cols], preferred_element_type=_F32)
        up = jnp.dot(h_ref[...], wu_ref[:, cols], preferred_element_type=_F32)
        act_ref[:, cols] = (gate * jax.nn.sigmoid(gate) * up).astype(act_ref.dtype)
        yield


def _post_kernel(x_ref, pool_ref, o1_ref, o2_ref, o3_ref, l1_ref, l2_ref, l3_ref, expand_ref,
                 wout_ref, gmix_ref, gpre_ref, wg_ref, wu_ref, wd_ref, gpost_ref, out_ref,
                 act_ref, x1_stage, h_stage, x1_prev, h_prev, staging):
    t = pl.program_id(0)

    @pl.when(t == 0)
    def _():
        x1_stage[...] = jnp.zeros_like(x1_stage)
        h_stage[...] = jnp.zeros_like(h_stage)

    def stage_tile():
        x1_prev[...] = x1_stage[...]
        yield
        h_prev[...] = h_stage[...]
        yield
        result = {}
        yield from _merge_stages(x_ref, pool_ref, (o1_ref, o2_ref, o3_ref), (l1_ref, l2_ref, l3_ref),
                                 expand_ref, wout_ref, staging, result)
        yield
        x1 = x_ref[0] + _rms_scale(result["mix"]) * gmix_ref[...]
        x1_stage[...] = x1
        h_stage[...] = (_rms_scale(x1) * gpre_ref[...]).astype(_BF16)

    _emit_interleaved(_gate_up_stages(h_stage, h_prev, wg_ref, wu_ref, act_ref), stage_tile())
    f = jnp.dot(act_ref[...], wd_ref[...], preferred_element_type=_F32)
    out_ref[0] = x1_prev[...] + _rms_scale(f) * gpost_ref[...]


def _post(x, pool, outs, lses, expand, w_out, gmix, gpre, wg, wu, wd, gpost):
    b, s, _ = x.shape
    tm = POST_TILE
    per_seq = s // tm
    n_tiles = b * per_seq
    merged = lambda t: jnp.minimum(t, n_tiles - 1)
    written = lambda t: jnp.maximum(t - 1, 0)
    row = lambda t: (merged(t) // per_seq, merged(t) % per_seq, 0)
    stream = lambda t: (merged(t) // per_seq, 0, 0, merged(t) % per_seq, 0)
    out_row = lambda t: (written(t) // per_seq, written(t) % per_seq, 0)
    return pl.pallas_call(
        _post_kernel,
        grid=(n_tiles + 1,),
        in_specs=[
            pl.BlockSpec((1, tm, D_MODEL), row),
            pl.BlockSpec((1, tm, POOL_WIDTH), row),
        ] + [pl.BlockSpec((1, N_PAIRS, d, tm // d, LANES), stream) for d in DILATIONS]
          + [pl.BlockSpec((1, PAIR_GROUPS, d, tm // d, LANES), stream) for d in DILATIONS] + [
            _resident((LANES, ATTN_WIDTH)),
            _resident((D_MODEL, D_MODEL)),
            _resident((1, D_MODEL)),
            _resident((1, D_MODEL)),
            _resident((D_MODEL, D_FF)),
            _resident((D_MODEL, D_FF)),
            _resident((D_FF, D_MODEL)),
            _resident((1, D_MODEL)),
        ],
        out_specs=pl.BlockSpec((1, tm, D_MODEL), out_row),
        out_shape=jax.ShapeDtypeStruct((b, s, D_MODEL), _F32),
        scratch_shapes=[
            pltpu.VMEM((tm, D_FF), _BF16),
            pltpu.VMEM((tm, D_MODEL), _F32),
            pltpu.VMEM((tm, D_MODEL), _BF16),
            pltpu.VMEM((tm, D_MODEL), _F32),
            pltpu.VMEM((tm, D_MODEL), _BF16),
            pltpu.VMEM((N_STAGING, tm, LANES), _F32),
        ],
        compiler_params=pltpu.CompilerParams(
            dimension_semantics=("arbitrary",), vmem_limit_bytes=VMEM_LIMIT),
        name="post",
    )(x, pool, *outs, *lses, expand, w_out, gmix, gpre, wg, wu, wd, gpost)


def _rope_tables(seq_len):
    freqs = ROPE_THETA ** (-np.arange(HALF_DIM, dtype=np.float64) * (2.0 / HEAD_DIM))
    ang = np.arange(seq_len, dtype=np.float64)[:, None] * freqs[None, :]
    cos, sin = np.cos(ang), np.sin(ang)
    cos_t = np.concatenate([cos, cos, cos, cos], axis=1)
    sin_t = np.concatenate([-sin, sin, -sin, sin], axis=1)
    return jnp.asarray(cos_t, dtype=_F32), jnp.asarray(sin_t, dtype=_F32)


def _block_diag(w_pool):
    n_g = w_pool.shape[0]
    out = jnp.zeros((POOL_WIDTH, POOL_WIDTH), w_pool.dtype)
    for gi in range(n_g):
        out = out.at[gi * POOL_GROUP:(gi + 1) * POOL_GROUP,
                     gi * POOL_GROUP:(gi + 1) * POOL_GROUP].set(w_pool[gi])
    return out


def _head_expand():
    e = np.zeros((LANES, ATTN_WIDTH), np.float32)
    for p in range(N_PAIRS):
        for hh in range(HEADS_PER_PAIR):
            h = HEADS_PER_PAIR * p + hh
            e[hh * HEAD_DIM + p, h * HEAD_DIM:(h + 1) * HEAD_DIM] = 1.0
    return jnp.asarray(e, dtype=_BF16)


def kernel(x, ln_pre_mix, w_in, w_pool, pool_scale, w_out, ln_post_mix, ln_pre_ffn,
           w_gate, w_up, w_down, ln_post_ffn):
    b, s, _ = x.shape
    depth = w_in.shape[0]
    cos_t, sin_t = _rope_tables(s)
    expand = _head_expand()
    for l in range(depth):
        pool, qs, ks, vs = _in_proj(
            x, ln_pre_mix[l][None, :], w_in[l].astype(_BF16), cos_t, sin_t,
            _block_diag(w_pool[l]).astype(_BF16), pool_scale[l][None, :])
        outs, lses = [], []
        for q, k, v in zip(qs, ks, vs):
            o, lse = _attention_config(q, k, v)
            outs.append(o)
            lses.append(lse)
        x = _post(x, pool, outs, lses, expand, w_out[l].astype(_BF16), ln_post_mix[l][None, :],
                  ln_pre_ffn[l][None, :], w_gate[l].astype(_BF16), w_up[l].astype(_BF16),
                  w_down[l].astype(_BF16), ln_post_ffn[l][None, :])
    return x
```

```python
import itertools
import math

import jax
import jax.numpy as jnp
import numpy as np
from jax import lax
from jax.experimental import pallas as pl
from jax.experimental.pallas import tpu as pltpu

D_MODEL = 1024
POOL_WIDTH = 256
POOL_WINDOWS = (2, 4, 8, 16)
POOL_GROUP = 64
HEAD_DIM = 64
HALF_DIM = HEAD_DIM // 2
ATTN_WIDTH = 768
N_HEADS = 12
DILATED_CONFIGS = ((128, 1), (512, 4), (2048, 16))
DILATIONS = tuple(d for _, d in DILATED_CONFIGS)
BLOCK = 128
ROPE_THETA = 10000.0
IN_WIDTH = POOL_WIDTH + 3 * ATTN_WIDTH
D_FF = 2816
EPS = 1e-6

LANES = 128
V7X_VMEM_BYTES = 64 * 1024 * 1024
MAX_POOL_WIN = max(POOL_WINDOWS)

N_PAIRS = ATTN_WIDTH // LANES
HEADS_PER_PAIR = LANES // HEAD_DIM

TOKEN_TILE = 1024
IN_SUBTILES = 4
POST_TILE = 512
ATTN_CHUNK = 2048
PAIR_GROUPS = 1
PAIRS_PER_GROUP = N_PAIRS // PAIR_GROUPS
FAST_STRIDE = 4
N_STAGING = 12
FF_CHUNK = 256
H_EARLY_CHUNKS = 3
VMEM_LIMIT = V7X_VMEM_BYTES * 7 // 8

LOG2E = math.log2(math.e)
LN2 = math.log(2.0)

_BF16 = jnp.bfloat16
_F32 = jnp.float32


def _rms_scale(x):
    return x * lax.rsqrt(jnp.mean(x * x, axis=-1, keepdims=True) + EPS)


def _resident(shape):
    nd = len(shape)
    return pl.BlockSpec(shape, lambda *_: (0,) * nd, pipeline_mode=pl.Buffered(1))


def _emit_interleaved(main, side):
    live = [g for g in (main, side) if g is not None]
    while live:
        for g in list(live):
            if next(g, StopIteration) is StopIteration:
                live.remove(g)


def _stream_shape(batch, lead, seq_len, dilation, dtype):
    return jax.ShapeDtypeStruct((batch, lead, dilation, seq_len // dilation, LANES), dtype)


def _rope_slab(xc, cos, sin_signed, first_half):
    ahead = pltpu.roll(xc, LANES - HALF_DIM, axis=1)
    behind = pltpu.roll(xc, HALF_DIM, axis=1)
    partner = jnp.where(first_half, ahead, behind)
    return xc * cos + partner * sin_signed


def _pool_mixer(u, first_pos, halo_ref, wpool_ref, pscale_ref):
    n = u.shape[0]
    ext = jnp.concatenate([halo_ref[...], u], axis=0)
    halo_ref[...] = u[n - MAX_POOL_WIN:, :]
    s2 = ext[1:, :] + ext[:-1, :]
    s4 = s2[2:, :] + s2[:-2, :]
    s8 = s4[4:, :] + s4[:-4, :]
    s16 = s8[8:, :] + s8[:-8, :]
    w2 = s2[MAX_POOL_WIN - 1:MAX_POOL_WIN - 1 + n, :]
    w4 = s4[MAX_POOL_WIN - 3:MAX_POOL_WIN - 3 + n, :]
    w8 = s8[MAX_POOL_WIN - 7:MAX_POOL_WIN - 7 + n, :]
    w16 = s16[1:1 + n, :]
    lane = lax.broadcasted_iota(jnp.int32, (n, POOL_WIDTH), 1)
    grp = lane // POOL_GROUP
    wsum = jnp.where(grp == 0, w2, jnp.where(grp == 1, w4, jnp.where(grp == 2, w8, w16)))
    win = jnp.where(grp == 0, POOL_WINDOWS[0], jnp.where(grp == 1, POOL_WINDOWS[1],
                    jnp.where(grp == 2, POOL_WINDOWS[2], POOL_WINDOWS[3])))
    pos = first_pos + lax.broadcasted_iota(jnp.int32, (n, POOL_WIDTH), 0)
    cnt = jnp.minimum(pos + 1, win).astype(_F32)
    d = wsum / cnt - u
    y = jnp.dot(d.astype(_BF16), wpool_ref[...], preferred_element_type=_F32)
    return y * pscale_ref[...]


def _store_streams(t, st_a, st_b, out_refs, p, s, n):
    ref1, ref4, ref16 = out_refs
    ref1[0, p, 0, s * n:(s + 1) * n, :] = t.astype(ref1.dtype)
    st_a[...] = t
    quarter = n // FAST_STRIDE
    for a in range(FAST_STRIDE):
        rows_a = st_a[pl.ds(a, quarter, stride=FAST_STRIDE), :]
        ref4[0, p, a, s * quarter:(s + 1) * quarter, :] = rows_a.astype(ref4.dtype)
        st_b[a * quarter:(a + 1) * quarter, :] = rows_a
    per_res = n // DILATIONS[2]
    for r in range(DILATIONS[2]):
        a, b = r % FAST_STRIDE, r // FAST_STRIDE
        ref16[0, p, r, s * per_res:(s + 1) * per_res, :] = (
            st_b[pl.ds(a * quarter + b, per_res, stride=FAST_STRIDE), :].astype(ref16.dtype))


def _project_stages(rows, x_ref, g_ref, w_ref, proj):
    h = (_rms_scale(x_ref[0, rows, :]) * g_ref[...]).astype(_BF16)
    yield
    edges = (0, POOL_WIDTH, POOL_WIDTH + ATTN_WIDTH, POOL_WIDTH + 2 * ATTN_WIDTH, IN_WIDTH)
    for name, lo, hi in zip(("u", "q", "k", "v"), edges[:-1], edges[1:]):
        proj[name] = jnp.dot(h, w_ref[:, lo:hi], preferred_element_type=_F32)
        yield


def _finish_stages(s, rows, first_pos, proj, cos_ref, sin_ref, wpool_ref, pscale_ref, halo_ref,
                   pool_ref, q_refs, k_refs, v_refs, staging):
    n = rows.stop - rows.start
    pool_ref[0, rows, :] = _pool_mixer(proj["u"], first_pos, halo_ref, wpool_ref,
                                       pscale_ref).astype(pool_ref.dtype)
    yield
    cos = cos_ref[rows, :]
    sin_signed = sin_ref[rows, :]
    lane128 = lax.broadcasted_iota(jnp.int32, (n, LANES), 1)
    first_half = (lane128 % HEAD_DIM) < HALF_DIM
    q_scale = (HEAD_DIM ** -0.5) * LOG2E
    slab_count = 0
    for name, refs, scale in (("q", q_refs, q_scale), ("k", k_refs, None), ("v", v_refs, None)):
        for p in range(N_PAIRS):
            t = proj[name][:, p * LANES:(p + 1) * LANES]
            if name != "v":
                t = _rope_slab(t, cos, sin_signed, first_half)
            if scale is not None:
                t = t * scale
            j = slab_count % (N_STAGING // 2)
            _store_streams(t, staging.at[2 * j], staging.at[2 * j + 1], refs, p, s, n)
            slab_count += 1
            if p % 2 == 1:
                yield


def _in_proj_kernel(x_ref, g_ref, w_ref, cos_ref, sin_ref, wpool_ref, pscale_ref, pool_ref, *rest):
    n_dil = len(DILATIONS)
    q_refs, k_refs, v_refs = rest[0:n_dil], rest[n_dil:2 * n_dil], rest[2 * n_dil:3 * n_dil]
    halo_ref, staging = rest[3 * n_dil:]
    i = pl.program_id(1)
    tm = x_ref.shape[1]
    sub = tm // IN_SUBTILES
    rows = [slice(s * sub, (s + 1) * sub) for s in range(IN_SUBTILES)]
    projs = [{} for _ in range(IN_SUBTILES)]

    @pl.when(i == 0)
    def _():
        halo_ref[...] = jnp.zeros_like(halo_ref)

    def project(s):
        return _project_stages(rows[s], x_ref, g_ref, w_ref, projs[s])

    def finish(s):
        return _finish_stages(s, rows[s], i * tm + s * sub, projs[s], cos_ref, sin_ref, wpool_ref,
                              pscale_ref, halo_ref, pool_ref, q_refs, k_refs, v_refs, staging)

    _emit_interleaved(project(0), None)
    for s in range(IN_SUBTILES):
        _emit_interleaved(project(s + 1) if s + 1 < IN_SUBTILES else None, finish(s))


def _in_proj(x, g, w_in, cos_t, sin_t, wpool_bd, pscale):
    b, s, _ = x.shape
    tm = TOKEN_TILE
    row = lambda bi, i: (bi, i, 0)
    tab = lambda bi, i: (i, 0)
    stream = lambda bi, i: (bi, 0, 0, i, 0)
    stream_specs = [pl.BlockSpec((1, N_PAIRS, d, tm // d, LANES), stream) for d in DILATIONS] * 3
    stream_shapes = [_stream_shape(b, N_PAIRS, s, d, _BF16) for d in DILATIONS] * 3
    outs = pl.pallas_call(
        _in_proj_kernel,
        grid=(b, s // tm),
        in_specs=[
            pl.BlockSpec((1, tm, D_MODEL), row),
            _resident((1, D_MODEL)),
            _resident((D_MODEL, IN_WIDTH)),
            pl.BlockSpec((tm, LANES), tab),
            pl.BlockSpec((tm, LANES), tab),
            _resident((POOL_WIDTH, POOL_WIDTH)),
            _resident((1, POOL_WIDTH)),
        ],
        out_specs=[pl.BlockSpec((1, tm, POOL_WIDTH), row)] + stream_specs,
        out_shape=[jax.ShapeDtypeStruct((b, s, POOL_WIDTH), _BF16)] + stream_shapes,
        scratch_shapes=[
            pltpu.VMEM((MAX_POOL_WIN, POOL_WIDTH), _F32),
            pltpu.VMEM((N_STAGING, tm // IN_SUBTILES, LANES), _F32),
        ],
        compiler_params=pltpu.CompilerParams(
            dimension_semantics=("arbitrary", "arbitrary"), vmem_limit_bytes=VMEM_LIMIT),
        name="in_proj",
    )(x, g, w_in, cos_t, sin_t, wpool_bd, pscale)
    n_dil = len(DILATIONS)
    return outs[0], outs[1:1 + n_dil], outs[1 + n_dil:1 + 2 * n_dil], outs[1 + 2 * n_dil:]


def _blk_rows(blk):
    return slice(blk * BLOCK, (blk + 1) * BLOCK)


def _attn_kernel(q_ref, k_ref, v_ref, kprev_ref, vprev_ref, o_ref, lse_ref, bias_ref):
    c = pl.program_id(2)
    n_pairs, dilation, rows_per_res = q_ref.shape[1:4]
    per_res = rows_per_res // BLOCK

    qi = lax.broadcasted_iota(jnp.int32, (BLOCK, 2 * BLOCK), 0)
    kj = lax.broadcasted_iota(jnp.int32, (BLOCK, 2 * BLOCK), 1)
    dist = qi + BLOCK - kj
    band = (dist >= 0) & (dist <= BLOCK)
    band_first = band & (kj >= BLOCK * (1 - jnp.minimum(c, 1)))
    bias_ref[0] = jnp.where(band_first, 0.0, -jnp.inf)
    bias_ref[1] = jnp.where(band, 0.0, -jnp.inf)
    lane = lax.broadcasted_iota(jnp.int32, (BLOCK, LANES), 1)
    low_head = lane < HEAD_DIM
    ones_block = jnp.ones((2 * BLOCK, LANES), _BF16)
    zero_bf16 = jnp.zeros((BLOCK, LANES), _BF16)

    def band_of(ref, prev_ref, p, r, i):
        head = prev_ref[0, p, r] if i == 0 else ref[0, p, r, _blk_rows(i - 1), :]
        return jnp.concatenate([head, ref[0, p, r, _blk_rows(i), :]], axis=0)

    def scores(p, r, i):
        q = q_ref[0, p, r, _blk_rows(i), :]
        q2 = jnp.concatenate([jnp.where(low_head, q, zero_bf16),
                              jnp.where(low_head, zero_bf16, q)], axis=0)
        return lax.dot_general(q2, band_of(k_ref, kprev_ref, p, r, i), (((1,), (1,)), ((), ())),
                               preferred_element_type=_F32)

    def attend(p, r, i, sc):
        bias = bias_ref.at[0 if i == 0 else 1]
        maxes, probs = [], []
        for hh in range(HEADS_PER_PAIR):
            s_h = sc[hh * BLOCK:(hh + 1) * BLOCK, :] + bias[...]
            m = jnp.max(s_h, axis=-1, keepdims=True)
            maxes.append(m)
            probs.append(jnp.exp2(s_h - m).astype(_BF16))
        v_aug = jnp.concatenate([band_of(v_ref, vprev_ref, p, r, i), ones_block], axis=1)
        pv = jnp.dot(jnp.concatenate(probs, axis=0), v_aug, preferred_element_type=_F32)
        num = jnp.where(low_head, pv[0:BLOCK, 0:LANES], pv[BLOCK:, 0:LANES])
        den = jnp.where(low_head, pv[0:BLOCK, LANES:], pv[BLOCK:, LANES:])
        o_ref[0, p, r, _blk_rows(i), :] = (num * (1.0 / den)).astype(o_ref.dtype)
        lse = jnp.where(low_head, maxes[0], maxes[1]) * LN2 + jnp.log(den)
        head_lanes = (lane % HEAD_DIM) == p
        if p == 0:
            lse_ref[0, 0, r, _blk_rows(i), :] = jnp.where(head_lanes, lse, 0.0)
        else:
            lse_ref[0, 0, r, _blk_rows(i), :] = jnp.where(head_lanes, lse,
                                                          lse_ref[0, 0, r, _blk_rows(i), :])

    work = [(p, r, i) for p in range(n_pairs) for r in range(dilation) for i in range(per_res)]
    sc_next = scores(*work[0])
    for n, item in enumerate(work):
        sc = sc_next
        if n + 1 < len(work):
            sc_next = scores(*work[n + 1])
        attend(*item, sc)


def _attention_config(q, k, v):
    b, _, dilation, length, _ = q.shape
    rows = ATTN_CHUNK // dilation
    pg = PAIRS_PER_GROUP
    stream = lambda bi, g, c: (bi, g, 0, c, 0)
    before = lambda bi, g, c: (bi, g, 0, jnp.maximum(c * (rows // BLOCK) - 1, 0), 0)
    return pl.pallas_call(
        _attn_kernel,
        grid=(b, PAIR_GROUPS, length // rows),
        in_specs=[pl.BlockSpec((1, pg, dilation, rows, LANES), stream)] * 3
        + [pl.BlockSpec((1, pg, dilation, BLOCK, LANES), before)] * 2,
        out_specs=(
            pl.BlockSpec((1, pg, dilation, rows, LANES), stream),
            pl.BlockSpec((1, 1, dilation, rows, LANES), stream),
        ),
        out_shape=(
            _stream_shape(b, N_PAIRS, dilation * length, dilation, _BF16),
            _stream_shape(b, PAIR_GROUPS, dilation * length, dilation, _F32),
        ),
        scratch_shapes=[pltpu.VMEM((2, BLOCK, 2 * BLOCK), _F32)],
        compiler_params=pltpu.CompilerParams(
            dimension_semantics=("arbitrary", "arbitrary", "arbitrary"),
            vmem_limit_bytes=VMEM_LIMIT),
        name=f"attn_d{dilation}",
    )(q, k, v, k, v)


def _position_order(slab_ref, st_a, st_b):
    dilation, per_res, _ = slab_ref.shape
    if dilation == 1:
        return slab_ref[0].astype(_F32)
    if dilation == FAST_STRIDE:
        for r in range(dilation):
            st_a[pl.ds(r, per_res, stride=FAST_STRIDE), :] = slab_ref[r].astype(_F32)
        return st_a[...]
    quarter = dilation * per_res // FAST_STRIDE
    for r in range(dilation):
        a, b = r % FAST_STRIDE, r // FAST_STRIDE
        st_b[pl.ds(a * quarter + b, per_res, stride=FAST_STRIDE), :] = slab_ref[r].astype(_F32)
    for a in range(FAST_STRIDE):
        st_a[pl.ds(a, quarter, stride=FAST_STRIDE), :] = st_b[a * quarter:(a + 1) * quarter, :]
    return st_a[...]


def _merge_stages(x_ref, pool_ref, o_refs, l_refs, expand_ref, w_ref, staging, result):
    tm = x_ref.shape[1]
    lane = lax.broadcasted_iota(jnp.int32, (tm, LANES), 1)
    buffers = itertools.count()

    def ordered(slab_ref):
        j = next(buffers) % (N_STAGING // 2)
        return _position_order(slab_ref, staging.at[2 * j], staging.at[2 * j + 1])

    def head_lse(l_ref):
        out = ordered(l_ref.at[0, 0])
        for g in range(1, PAIR_GROUPS):
            shifted = pltpu.roll(ordered(l_ref.at[0, g]), g * PAIRS_PER_GROUP, axis=1)
            out = jnp.where((lane % HEAD_DIM) < g * PAIRS_PER_GROUP, out, shifted)
        return out

    l1, l2, l3 = (head_lse(l_ref) for l_ref in l_refs)
    mx = jnp.maximum(jnp.maximum(l1, l2), l3)
    e1, e2, e3 = jnp.exp(l1 - mx), jnp.exp(l2 - mx), jnp.exp(l3 - mx)
    inv = 1.0 / (e1 + e2 + e3)
    weights = [(e * inv).astype(_BF16) for e in (e1, e2)]
    yield
    wexps = [jnp.dot(w, expand_ref[...], preferred_element_type=_F32) for w in weights]
    wexps.append(1.0 - wexps[0] - wexps[1])
    mix_pool = jnp.dot(pool_ref[0], w_ref[0:POOL_WIDTH, :], preferred_element_type=_F32)
    yield
    attn = None
    for wexp, o_ref in zip(wexps, o_refs):
        o = jnp.concatenate([ordered(o_ref.at[0, p]) for p in range(N_PAIRS)], axis=1)
        term = wexp * o
        attn = term if attn is None else attn + term
        yield
    result["mix"] = mix_pool + jnp.dot(attn.astype(_BF16), w_ref[POOL_WIDTH:, :],
                                       preferred_element_type=_F32)


def _post_kernel(x_ref, pool_ref, o1_ref, o2_ref, o3_ref, l1_ref, l2_ref, l3_ref, expand_ref,
                 wout_ref, gmix_ref, gpre_ref, wg_ref, wu_ref, wd_ref, gpost_ref, out_ref,
                 act_ref, staging):
    result = {}
    for _ in _merge_stages(x_ref, pool_ref, (o1_ref, o2_ref, o3_ref), (l1_ref, l2_ref, l3_ref),
                           expand_ref, wout_ref, staging, result):
        pass
    x1 = x_ref[0] + _rms_scale(result["mix"]) * gmix_ref[...]
    h = (_rms_scale(x1) * gpre_ref[...]).astype(_BF16)
    for c in range(D_FF // FF_CHUNK):
        cols = slice(c * FF_CHUNK, (c + 1) * FF_CHUNK)
        gate = jnp.dot(h, wg_ref[:, cols], preferred_element_type=_F32)
        up = jnp.dot(h, wu_ref[:, cols], preferred_element_type=_F32)
        act_ref[:, cols] = (gate * jax.nn.sigmoid(gate) * up).astype(act_ref.dtype)
    f = jnp.dot(act_ref[...], wd_ref[...], preferred_element_type=_F32)
    out_ref[0] = x1 + _rms_scale(f) * gpost_ref[...]


def _post(x, pool, outs, lses, expand, w_out, gmix, gpre, wg, wu, wd, gpost):
    b, s, _ = x.shape
    tm = POST_TILE
    row = lambda bi, i: (bi, i, 0)
    stream = lambda bi, i: (bi, 0, 0, i, 0)
    out_row = row
    return pl.pallas_call(
        _post_kernel,
        grid=(b, s // tm),
        in_specs=[
            pl.BlockSpec((1, tm, D_MODEL), row),
            pl.BlockSpec((1, tm, POOL_WIDTH), row),
        ] + [pl.BlockSpec((1, N_PAIRS, d, tm // d, LANES), stream) for d in DILATIONS]
          + [pl.BlockSpec((1, PAIR_GROUPS, d, tm // d, LANES), stream) for d in DILATIONS] + [
            _resident((LANES, ATTN_WIDTH)),
            _resident((D_MODEL, D_MODEL)),
            _resident((1, D_MODEL)),
            _resident((1, D_MODEL)),
            _resident((D_MODEL, D_FF)),
            _resident((D_MODEL, D_FF)),
            _resident((D_FF, D_MODEL)),
            _resident((1, D_MODEL)),
        ],
        out_specs=pl.BlockSpec((1, tm, D_MODEL), out_row),
        out_shape=jax.ShapeDtypeStruct((b, s, D_MODEL), _F32),
        scratch_shapes=[
            pltpu.VMEM((tm, D_FF), _BF16),
            pltpu.VMEM((N_STAGING, tm, LANES), _F32),
        ],
        compiler_params=pltpu.CompilerParams(
            dimension_semantics=("arbitrary", "arbitrary"), vmem_limit_bytes=VMEM_LIMIT),
        name="post",
    )(x, pool, *outs, *lses, expand, w_out, gmix, gpre, wg, wu, wd, gpost)


def _rope_tables(seq_len):
    freqs = ROPE_THETA ** (-np.arange(HALF_DIM, dtype=np.float64) * (2.0 / HEAD_DIM))
    ang = np.arange(seq_len, dtype=np.float64)[:, None] * freqs[None, :]
    cos, sin = np.cos(ang), np.sin(ang)
    cos_t = np.concatenate([cos, cos, cos, cos], axis=1)
    sin_t = np.concatenate([-sin, sin, -sin, sin], axis=1)
    return jnp.asarray(cos_t, dtype=_F32), jnp.asarray(sin_t, dtype=_F32)


def _block_diag(w_pool):
    n_g = w_pool.shape[0]
    out = jnp.zeros((POOL_WIDTH, POOL_WIDTH), w_pool.dtype)
    for gi in range(n_g):
        out = out.at[gi * POOL_GROUP:(gi + 1) * POOL_GROUP,
                     gi * POOL_GROUP:(gi + 1) * POOL_GROUP].set(w_pool[gi])
    return out


def _head_expand():
    e = np.zeros((LANES, ATTN_WIDTH), np.float32)
    for p in range(N_PAIRS):
        for hh in range(HEADS_PER_PAIR):
            h = HEADS_PER_PAIR * p + hh
            e[hh * HEAD_DIM + p, h * HEAD_DIM:(h + 1) * HEAD_DIM] = 1.0
    return jnp.asarray(e, dtype=_BF16)


def kernel(x, ln_pre_mix, w_in, w_pool, pool_scale, w_out, ln_post_mix, ln_pre_ffn,
           w_gate, w_up, w_down, ln_post_ffn):
    b, s, _ = x.shape
    depth = w_in.shape[0]
    cos_t, sin_t = _rope_tables(s)
    expand = _head_expand()
    for l in range(depth):
        pool, qs, ks, vs = _in_proj(
            x, ln_pre_mix[l][None, :], w_in[l].astype(_BF16), cos_t, sin_t,
            _block_diag(w_pool[l]).astype(_BF16), pool_scale[l][None, :])
        outs, lses = [], []
        for q, k, v in zip(qs, ks, vs):
            o, lse = _attention_config(q, k, v)
            outs.append(o)
            lses.append(lse)
        x = _post(x, pool, outs, lses, expand, w_out[l].astype(_BF16), ln_post_mix[l][None, :],
                  ln_pre_ffn[l][None, :], w_gate[l].astype(_BF16), w_up[l].astype(_BF16),
                  w_down[l].astype(_BF16), ln_post_ffn[l][None, :])
    return x
```

```python
import itertools
import math

import jax
import jax.numpy as jnp
import numpy as np
from jax import lax
from jax.experimental import pallas as pl
from jax.experimental.pallas import tpu as pltpu

D_MODEL = 1024
POOL_WIDTH = 256
POOL_WINDOWS = (2, 4, 8, 16)
POOL_GROUP = 64
HEAD_DIM = 64
HALF_DIM = HEAD_DIM // 2
ATTN_WIDTH = 768
N_HEADS = 12
DILATED_CONFIGS = ((128, 1), (512, 4), (2048, 16))
DILATIONS = tuple(d for _, d in DILATED_CONFIGS)
BLOCK = 128
ROPE_THETA = 10000.0
IN_WIDTH = POOL_WIDTH + 3 * ATTN_WIDTH
D_FF = 2816
EPS = 1e-6

LANES = 128
V7X_VMEM_BYTES = 64 * 1024 * 1024
MAX_POOL_WIN = max(POOL_WINDOWS)

N_PAIRS = ATTN_WIDTH // LANES
HEADS_PER_PAIR = LANES // HEAD_DIM

TOKEN_TILE = 1024
IN_SUBTILES = 4
POST_TILE = 512
ATTN_CHUNK = 2048
PAIR_GROUPS = 1
PAIRS_PER_GROUP = N_PAIRS // PAIR_GROUPS
FAST_STRIDE = 4
N_STAGING = 12
FF_CHUNK = 256
VMEM_LIMIT = V7X_VMEM_BYTES * 7 // 8

LOG2E = math.log2(math.e)
LN2 = math.log(2.0)

_BF16 = jnp.bfloat16
_F32 = jnp.float32


def _rms_scale(x):
    return x * lax.rsqrt(jnp.mean(x * x, axis=-1, keepdims=True) + EPS)


def _resident(shape):
    nd = len(shape)
    return pl.BlockSpec(shape, lambda *_: (0,) * nd, pipeline_mode=pl.Buffered(1))


def _emit_interleaved(main, side):
    live = [g for g in (main, side) if g is not None]
    while live:
        for g in list(live):
            if next(g, StopIteration) is StopIteration:
                live.remove(g)


def _stream_shape(batch, lead, seq_len, dilation, dtype):
    return jax.ShapeDtypeStruct((batch, lead, dilation, seq_len // dilation, LANES), dtype)


def _rope_slab(xc, cos, sin_signed, first_half):
    ahead = pltpu.roll(xc, LANES - HALF_DIM, axis=1)
    behind = pltpu.roll(xc, HALF_DIM, axis=1)
    partner = jnp.where(first_half, ahead, behind)
    return xc * cos + partner * sin_signed


def _pool_mixer(u, first_pos, halo_ref, wpool_ref, pscale_ref):
    n = u.shape[0]
    ext = jnp.concatenate([halo_ref[...], u], axis=0)
    halo_ref[...] = u[n - MAX_POOL_WIN:, :]
    s2 = ext[1:, :] + ext[:-1, :]
    s4 = s2[2:, :] + s2[:-2, :]
    s8 = s4[4:, :] + s4[:-4, :]
    s16 = s8[8:, :] + s8[:-8, :]
    w2 = s2[MAX_POOL_WIN - 1:MAX_POOL_WIN - 1 + n, :]
    w4 = s4[MAX_POOL_WIN - 3:MAX_POOL_WIN - 3 + n, :]
    w8 = s8[MAX_POOL_WIN - 7:MAX_POOL_WIN - 7 + n, :]
    w16 = s16[1:1 + n, :]
    lane = lax.broadcasted_iota(jnp.int32, (n, POOL_WIDTH), 1)
    grp = lane // POOL_GROUP
    wsum = jnp.where(grp == 0, w2, jnp.where(grp == 1, w4, jnp.where(grp == 2, w8, w16)))
    win = jnp.where(grp == 0, POOL_WINDOWS[0], jnp.where(grp == 1, POOL_WINDOWS[1],
                    jnp.where(grp == 2, POOL_WINDOWS[2], POOL_WINDOWS[3])))
    pos = first_pos + lax.broadcasted_iota(jnp.int32, (n, POOL_WIDTH), 0)
    cnt = jnp.minimum(pos + 1, win).astype(_F32)
    d = wsum / cnt - u
    y = jnp.dot(d.astype(_BF16), wpool_ref[...], preferred_element_type=_F32)
    return y * pscale_ref[...]


def _store_streams(t, st_a, st_b, out_refs, p, s, n):
    ref1, ref4, ref16 = out_refs
    ref1[0, p, 0, s * n:(s + 1) * n, :] = t.astype(ref1.dtype)
    st_a[...] = t
    quarter = n // FAST_STRIDE
    for a in range(FAST_STRIDE):
        rows_a = st_a[pl.ds(a, quarter, stride=FAST_STRIDE), :]
        ref4[0, p, a, s * quarter:(s + 1) * quarter, :] = rows_a.astype(ref4.dtype)
        st_b[a * quarter:(a + 1) * quarter, :] = rows_a
    per_res = n // DILATIONS[2]
    for r in range(DILATIONS[2]):
        a, b = r % FAST_STRIDE, r // FAST_STRIDE
        ref16[0, p, r, s * per_res:(s + 1) * per_res, :] = (
            st_b[pl.ds(a * quarter + b, per_res, stride=FAST_STRIDE), :].astype(ref16.dtype))


def _project_stages(rows, x_ref, g_ref, w_ref, proj):
    h = (_rms_scale(x_ref[0, rows, :]) * g_ref[...]).astype(_BF16)
    yield
    edges = (0, POOL_WIDTH, POOL_WIDTH + ATTN_WIDTH, POOL_WIDTH + 2 * ATTN_WIDTH, IN_WIDTH)
    for name, lo, hi in zip(("u", "q", "k", "v"), edges[:-1], edges[1:]):
        proj[name] = jnp.dot(h, w_ref[:, lo:hi], preferred_element_type=_F32)
        yield


def _finish_stages(s, rows, first_pos, proj, cos_ref, sin_ref, wpool_ref, pscale_ref, halo_ref,
                   pool_ref, q_refs, k_refs, v_refs, staging):
    n = rows.stop - rows.start
    pool_ref[0, rows, :] = _pool_mixer(proj["u"], first_pos, halo_ref, wpool_ref,
                                       pscale_ref).astype(pool_ref.dtype)
    yield
    cos = cos_ref[rows, :]
    sin_signed = sin_ref[rows, :]
    lane128 = lax.broadcasted_iota(jnp.int32, (n, LANES), 1)
    first_half = (lane128 % HEAD_DIM) < HALF_DIM
    q_scale = (HEAD_DIM ** -0.5) * LOG2E
    slab_count = 0
    for name, refs, scale in (("q", q_refs, q_scale), ("k", k_refs, None), ("v", v_refs, None)):
        for p in range(N_PAIRS):
            t = proj[name][:, p * LANES:(p + 1) * LANES]
            if name != "v":
                t = _rope_slab(t, cos, sin_signed, first_half)
            if scale is not None:
                t = t * scale
            j = slab_count % (N_STAGING // 2)
            _store_streams(t, staging.at[2 * j], staging.at[2 * j + 1], refs, p, s, n)
            slab_count += 1
            if p % 2 == 1:
                yield


def _in_proj_kernel(x_ref, g_ref, w_ref, cos_ref, sin_ref, wpool_ref, pscale_ref, pool_ref, *rest):
    n_dil = len(DILATIONS)
    q_refs, k_refs, v_refs = rest[0:n_dil], rest[n_dil:2 * n_dil], rest[2 * n_dil:3 * n_dil]
    halo_ref, staging = rest[3 * n_dil:]
    i = pl.program_id(1)
    tm = x_ref.shape[1]
    sub = tm // IN_SUBTILES
    rows = [slice(s * sub, (s + 1) * sub) for s in range(IN_SUBTILES)]
    projs = [{} for _ in range(IN_SUBTILES)]

    @pl.when(i == 0)
    def _():
        halo_ref[...] = jnp.zeros_like(halo_ref)

    def project(s):
        return _project_stages(rows[s], x_ref, g_ref, w_ref, projs[s])

    def finish(s):
        return _finish_stages(s, rows[s], i * tm + s * sub, projs[s], cos_ref, sin_ref, wpool_ref,
                              pscale_ref, halo_ref, pool_ref, q_refs, k_refs, v_refs, staging)

    _emit_interleaved(project(0), None)
    for s in range(IN_SUBTILES):
        _emit_interleaved(project(s + 1) if s + 1 < IN_SUBTILES else None, finish(s))


def _in_proj(x, g, w_in, cos_t, sin_t, wpool_bd, pscale):
    b, s, _ = x.shape
    tm = TOKEN_TILE
    row = lambda bi, i: (bi, i, 0)
    tab = lambda bi, i: (i, 0)
    stream = lambda bi, i: (bi, 0, 0, i, 0)
    stream_specs = [pl.BlockSpec((1, N_PAIRS, d, tm // d, LANES), stream) for d in DILATIONS] * 3
    stream_shapes = [_stream_shape(b, N_PAIRS, s, d, _BF16) for d in DILATIONS] * 3
    outs = pl.pallas_call(
        _in_proj_kernel,
        grid=(b, s // tm),
        in_specs=[
            pl.BlockSpec((1, tm, D_MODEL), row),
            _resident((1, D_MODEL)),
            _resident((D_MODEL, IN_WIDTH)),
            pl.BlockSpec((tm, LANES), tab),
            pl.BlockSpec((tm, LANES), tab),
            _resident((POOL_WIDTH, POOL_WIDTH)),
            _resident((1, POOL_WIDTH)),
        ],
        out_specs=[pl.BlockSpec((1, tm, POOL_WIDTH), row)] + stream_specs,
        out_shape=[jax.ShapeDtypeStruct((b, s, POOL_WIDTH), _BF16)] + stream_shapes,
        scratch_shapes=[
            pltpu.VMEM((MAX_POOL_WIN, POOL_WIDTH), _F32),
            pltpu.VMEM((N_STAGING, tm // IN_SUBTILES, LANES), _F32),
        ],
        compiler_params=pltpu.CompilerParams(
            dimension_semantics=("arbitrary", "arbitrary"), vmem_limit_bytes=VMEM_LIMIT),
        name="in_proj",
    )(x, g, w_in, cos_t, sin_t, wpool_bd, pscale)
    n_dil = len(DILATIONS)
    return outs[0], outs[1:1 + n_dil], outs[1 + n_dil:1 + 2 * n_dil], outs[1 + 2 * n_dil:]


def _blk_rows(blk):
    return slice(blk * BLOCK, (blk + 1) * BLOCK)


def _attn_kernel(q_ref, k_ref, v_ref, kprev_ref, vprev_ref, o_ref, lse_ref, bias_ref):
    c = pl.program_id(2)
    n_pairs, dilation, rows_per_res = q_ref.shape[1:4]
    per_res = rows_per_res // BLOCK

    qi = lax.broadcasted_iota(jnp.int32, (BLOCK, 2 * BLOCK), 0)
    kj = lax.broadcasted_iota(jnp.int32, (BLOCK, 2 * BLOCK), 1)
    dist = qi + BLOCK - kj
    band = (dist >= 0) & (dist <= BLOCK)
    band_first = band & (kj >= BLOCK * (1 - jnp.minimum(c, 1)))
    bias_ref[0] = jnp.where(band_first, 0.0, -jnp.inf)
    bias_ref[1] = jnp.where(band, 0.0, -jnp.inf)
    lane = lax.broadcasted_iota(jnp.int32, (BLOCK, LANES), 1)
    low_head = lane < HEAD_DIM
    ones_block = jnp.ones((2 * BLOCK, LANES), _BF16)
    zero_bf16 = jnp.zeros((BLOCK, LANES), _BF16)

    def band_of(ref, prev_ref, p, r, i):
        head = prev_ref[0, p, r] if i == 0 else ref[0, p, r, _blk_rows(i - 1), :]
        return jnp.concatenate([head, ref[0, p, r, _blk_rows(i), :]], axis=0)

    def scores(p, r, i):
        q = q_ref[0, p, r, _blk_rows(i), :]
        q2 = jnp.concatenate([jnp.where(low_head, q, zero_bf16),
                              jnp.where(low_head, zero_bf16, q)], axis=0)
        return lax.dot_general(q2, band_of(k_ref, kprev_ref, p, r, i), (((1,), (1,)), ((), ())),
                               preferred_element_type=_F32)

    def attend(p, r, i, sc):
        bias = bias_ref.at[0 if i == 0 else 1]
        maxes, probs = [], []
        for hh in range(HEADS_PER_PAIR):
            s_h = sc[hh * BLOCK:(hh + 1) * BLOCK, :] + bias[...]
            m = jnp.max(s_h, axis=-1, keepdims=True)
            maxes.append(m)
            probs.append(jnp.exp2(s_h - m).astype(_BF16))
        v_aug = jnp.concatenate([band_of(v_ref, vprev_ref, p, r, i), ones_block], axis=1)
        pv = jnp.dot(jnp.concatenate(probs, axis=0), v_aug, preferred_element_type=_F32)
        num = jnp.where(low_head, pv[0:BLOCK, 0:LANES], pv[BLOCK:, 0:LANES])
        den = jnp.where(low_head, pv[0:BLOCK, LANES:], pv[BLOCK:, LANES:])
        o_ref[0, p, r, _blk_rows(i), :] = (num * (1.0 / den)).astype(o_ref.dtype)
        lse = jnp.where(low_head, maxes[0], maxes[1]) * LN2 + jnp.log(den)
        head_lanes = (lane % HEAD_DIM) == p
        if p == 0:
            lse_ref[0, 0, r, _blk_rows(i), :] = jnp.where(head_lanes, lse, 0.0)
        else:
            lse_ref[0, 0, r, _blk_rows(i), :] = jnp.where(head_lanes, lse,
                                                          lse_ref[0, 0, r, _blk_rows(i), :])

    work = [(p, r, i) for p in range(n_pairs) for r in range(dilation) for i in range(per_res)]
    sc_next = scores(*work[0])
    for n, item in enumerate(work):
        sc = sc_next
        if n + 1 < len(work):
            sc_next = scores(*work[n + 1])
        attend(*item, sc)


def _attention_config(q, k, v):
    b, _, dilation, length, _ = q.shape
    rows = ATTN_CHUNK // dilation
    pg = PAIRS_PER_GROUP
    stream = lambda bi, g, c: (bi, g, 0, c, 0)
    before = lambda bi, g, c: (bi, g, 0, jnp.maximum(c * (rows // BLOCK) - 1, 0), 0)
    return pl.pallas_call(
        _attn_kernel,
        grid=(b, PAIR_GROUPS, length // rows),
        in_specs=[pl.BlockSpec((1, pg, dilation, rows, LANES), stream)] * 3
        + [pl.BlockSpec((1, pg, dilation, BLOCK, LANES), before)] * 2,
        out_specs=(
            pl.BlockSpec((1, pg, dilation, rows, LANES), stream),
            pl.BlockSpec((1, 1, dilation, rows, LANES), stream),
        ),
        out_shape=(
            _stream_shape(b, N_PAIRS, dilation * length, dilation, _BF16),
            _stream_shape(b, PAIR_GROUPS, dilation * length, dilation, _F32),
        ),
        scratch_shapes=[pltpu.VMEM((2, BLOCK, 2 * BLOCK), _F32)],
        compiler_params=pltpu.CompilerParams(
            dimension_semantics=("arbitrary", "arbitrary", "arbitrary"),
            vmem_limit_bytes=VMEM_LIMIT),
        name=f"attn_d{dilation}",
    )(q, k, v, k, v)


def _position_order(slab_ref, st_a, st_b):
    dilation, per_res, _ = slab_ref.shape
    if dilation == 1:
        return slab_ref[0].astype(_F32)
    if dilation == FAST_STRIDE:
        for r in range(dilation):
            st_a[pl.ds(r, per_res, stride=FAST_STRIDE), :] = slab_ref[r].astype(_F32)
        return st_a[...]
    quarter = dilation * per_res // FAST_STRIDE
    for r in range(dilation):
        a, b = r % FAST_STRIDE, r // FAST_STRIDE
        st_b[pl.ds(a * quarter + b, per_res, stride=FAST_STRIDE), :] = slab_ref[r].astype(_F32)
    for a in range(FAST_STRIDE):
        st_a[pl.ds(a, quarter, stride=FAST_STRIDE), :] = st_b[a * quarter:(a + 1) * quarter, :]
    return st_a[...]


def _merge_and_project(x_ref, pool_ref, o_refs, l_refs, expand_ref, w_ref, staging):
    tm = x_ref.shape[1]
    lane = lax.broadcasted_iota(jnp.int32, (tm, LANES), 1)
    buffers = itertools.count()

    def ordered(slab_ref):
        j = next(buffers) % (N_STAGING // 2)
        return _position_order(slab_ref, staging.at[2 * j], staging.at[2 * j + 1])

    def head_lse(l_ref):
        out = ordered(l_ref.at[0, 0])
        for g in range(1, PAIR_GROUPS):
            shifted = pltpu.roll(ordered(l_ref.at[0, g]), g * PAIRS_PER_GROUP, axis=1)
            out = jnp.where((lane % HEAD_DIM) < g * PAIRS_PER_GROUP, out, shifted)
        return out

    l1, l2, l3 = (head_lse(l_ref) for l_ref in l_refs)
    mx = jnp.maximum(jnp.maximum(l1, l2), l3)
    e1, e2, e3 = jnp.exp(l1 - mx), jnp.exp(l2 - mx), jnp.exp(l3 - mx)
    inv = 1.0 / (e1 + e2 + e3)
    weights = [(e * inv).astype(_BF16) for e in (e1, e2)]
    wexps = [jnp.dot(w, expand_ref[...], preferred_element_type=_F32) for w in weights]
    wexps.append(1.0 - wexps[0] - wexps[1])
    mix_pool = jnp.dot(pool_ref[0], w_ref[0:POOL_WIDTH, :], preferred_element_type=_F32)
    attn = None
    for wexp, o_ref in zip(wexps, o_refs):
        o = jnp.concatenate([ordered(o_ref.at[0, p]) for p in range(N_PAIRS)], axis=1)
        term = wexp * o
        attn = term if attn is None else attn + term
    return mix_pool + jnp.dot(attn.astype(_BF16), w_ref[POOL_WIDTH:, :],
                              preferred_element_type=_F32)


def _post_kernel(x_ref, pool_ref, o1_ref, o2_ref, o3_ref, l1_ref, l2_ref, l3_ref, expand_ref,
                 wout_ref, gmix_ref, gpre_ref, wg_ref, wu_ref, wd_ref, gpost_ref, out_ref,
                 act_ref, staging):
    mix = _merge_and_project(x_ref, pool_ref, (o1_ref, o2_ref, o3_ref), (l1_ref, l2_ref, l3_ref),
                             expand_ref, wout_ref, staging)
    x1 = x_ref[0] + _rms_scale(mix) * gmix_ref[...]
    h = (_rms_scale(x1) * gpre_ref[...]).astype(_BF16)
    for c in range(D_FF // FF_CHUNK):
        cols = slice(c * FF_CHUNK, (c + 1) * FF_CHUNK)
        gate = jnp.dot(h, wg_ref[:, cols], preferred_element_type=_F32)
        up = jnp.dot(h, wu_ref[:, cols], preferred_element_type=_F32)
        act_ref[:, cols] = (gate * jax.nn.sigmoid(gate) * up).astype(act_ref.dtype)
    f = jnp.dot(act_ref[...], wd_ref[...], preferred_element_type=_F32)
    out_ref[0] = x1 + _rms_scale(f) * gpost_ref[...]


def _post(x, pool, outs, lses, expand, w_out, gmix, gpre, wg, wu, wd, gpost):
    b, s, _ = x.shape
    tm = POST_TILE
    row = lambda bi, i: (bi, i, 0)
    stream = lambda bi, i: (bi, 0, 0, i, 0)
    out_row = row
    return pl.pallas_call(
        _post_kernel,
        grid=(b, s // tm),
        in_specs=[
            pl.BlockSpec((1, tm, D_MODEL), row),
            pl.BlockSpec((1, tm, POOL_WIDTH), row),
        ] + [pl.BlockSpec((1, N_PAIRS, d, tm // d, LANES), stream) for d in DILATIONS]
          + [pl.BlockSpec((1, PAIR_GROUPS, d, tm // d, LANES), stream) for d in DILATIONS] + [
            _resident((LANES, ATTN_WIDTH)),
            _resident((D_MODEL, D_MODEL)),
            _resident((1, D_MODEL)),
            _resident((1, D_MODEL)),
            _resident((D_MODEL, D_FF)),
            _resident((D_MODEL, D_FF)),
            _resident((D_FF, D_MODEL)),
            _resident((1, D_MODEL)),
        ],
        out_specs=pl.BlockSpec((1, tm, D_MODEL), out_row),
        out_shape=jax.ShapeDtypeStruct((b, s, D_MODEL), _F32),
        scratch_shapes=[
            pltpu.VMEM((tm, D_FF), _BF16),
            pltpu.VMEM((N_STAGING, tm, LANES), _F32),
        ],
        compiler_params=pltpu.CompilerParams(
            dimension_semantics=("arbitrary", "arbitrary"), vmem_limit_bytes=VMEM_LIMIT),
        name="post",
    )(x, pool, *outs, *lses, expand, w_out, gmix, gpre, wg, wu, wd, gpost)


def _rope_tables(seq_len):
    freqs = ROPE_THETA ** (-np.arange(HALF_DIM, dtype=np.float64) * (2.0 / HEAD_DIM))
    ang = np.arange(seq_len, dtype=np.float64)[:, None] * freqs[None, :]
    cos, sin = np.cos(ang), np.sin(ang)
    cos_t = np.concatenate([cos, cos, cos, cos], axis=1)
    sin_t = np.concatenate([-sin, sin, -sin, sin], axis=1)
    return jnp.asarray(cos_t, dtype=_F32), jnp.asarray(sin_t, dtype=_F32)


def _block_diag(w_pool):
    n_g = w_pool.shape[0]
    out = jnp.zeros((POOL_WIDTH, POOL_WIDTH), w_pool.dtype)
    for gi in range(n_g):
        out = out.at[gi * POOL_GROUP:(gi + 1) * POOL_GROUP,
                     gi * POOL_GROUP:(gi + 1) * POOL_GROUP].set(w_pool[gi])
    return out


def _head_expand():
    e = np.zeros((LANES, ATTN_WIDTH), np.float32)
    for p in range(N_PAIRS):
        for hh in range(HEADS_PER_PAIR):
            h = HEADS_PER_PAIR * p + hh
            e[hh * HEAD_DIM + p, h * HEAD_DIM:(h + 1) * HEAD_DIM] = 1.0
    return jnp.asarray(e, dtype=_BF16)


def kernel(x, ln_pre_mix, w_in, w_pool, pool_scale, w_out, ln_post_mix, ln_pre_ffn,
           w_gate, w_up, w_down, ln_post_ffn):
    b, s, _ = x.shape
    depth = w_in.shape[0]
    cos_t, sin_t = _rope_tables(s)
    expand = _head_expand()
    for l in range(depth):
        pool, qs, ks, vs = _in_proj(
            x, ln_pre_mix[l][None, :], w_in[l].astype(_BF16), cos_t, sin_t,
            _block_diag(w_pool[l]).astype(_BF16), pool_scale[l][None, :])
        outs, lses = [], []
        for q, k, v in zip(qs, ks, vs):
            o, lse = _attention_config(q, k, v)
            outs.append(o)
            lses.append(lse)
        x = _post(x, pool, outs, lses, expand, w_out[l].astype(_BF16), ln_post_mix[l][None, :],
                  ln_pre_ffn[l][None, :], w_gate[l].astype(_BF16), w_up[l].astype(_BF16),
                  w_down[l].astype(_BF16), ln_post_ffn[l][None, :])
    return x
```
